```python
import math
import jax
import jax.numpy as jnp
from jax import lax
import numpy as np

D_MODEL = 1024
BATCH = 2
SEQ = 8192
DEPTH = 4
DEC_BATCH = 128
DEC_SEQ = 8
PAST_LEN = 2048
PAGE_SIZE = 128

HA = 4
DK = 64
DVA = 2 * DK
HB = 8
DH = 64
D_FF = 2816
N_EXPERTS = 8
TOP_K = 2
N_DENSE = (DEPTH + 1) // 2
N_MOE = DEPTH // 2
NUM_BUCKETS = 32
MAX_DISTANCE = 128
Q_BLOCK = 128
EPS = 1e-6
NEG_INF = -1e30
COL_SIZES = (HA * 2 * DK, HA * 2 * DK, HA * DVA, HB * DH, HB * DH, HB * DH, HB, D_MODEL, D_MODEL)
IN_COLS = 2 * HA * 2 * DK + HA * DVA + 3 * HB * DH + HB + 2 * D_MODEL

kernel_name = 'gated_diff_fox_adaln_moe_decoder_step'


def rmsnorm(x, g):
    x32 = x.astype(jnp.float32)
    y = x32 * lax.rsqrt(jnp.mean(x32 * x32, axis=-1, keepdims=True) + EPS)
    return (y * g.astype(jnp.float32)).astype(x.dtype)


def t5_bucket(rel):
    n = jnp.maximum(rel, 0)
    max_exact = NUM_BUCKETS // 2
    nf = jnp.maximum(n, 1).astype(jnp.float32)
    large = max_exact + (jnp.log(nf / max_exact) / math.log(MAX_DISTANCE / max_exact)
                         * (NUM_BUCKETS - max_exact)).astype(jnp.int32)
    large = jnp.minimum(large, NUM_BUCKETS - 1)
    return jnp.where(n < max_exact, n, large)


def split_blocks(a, nb):
    return jnp.moveaxis(a.reshape((a.shape[0], nb, a.shape[1] // nb) + a.shape[2:]), 1, 0)


def merge_blocks(a):
    a = jnp.moveaxis(a, 0, 1)
    return a.reshape((a.shape[0], a.shape[1] * a.shape[2]) + a.shape[3:])


def gather_pages(cache_l, page_table):
    g = cache_l[page_table]
    return g.reshape((g.shape[0], g.shape[1] * g.shape[2]) + g.shape[3:])


def diff_attention(q, k, v, q_pos, k_pos, rel_bias, lam):
    T = q.shape[1]
    tb = min(Q_BLOCK, T)
    nb = T // tb
    scale = DK ** -0.5
    table = rel_bias.astype(jnp.float32)

    def block(args):
        qb, pb = args
        rel = pb[:, None] - k_pos[None, :]
        bias = jnp.transpose(table[t5_bucket(rel)], (2, 0, 1))
        s = jnp.einsum('bthid,bshid->bihts', qb, k).astype(jnp.float32) * scale + bias
        s = jnp.where(rel >= 0, s, NEG_INF)
        pr = jax.nn.softmax(s, axis=-1)
        a = pr[:, 0] - lam * pr[:, 1]
        return jnp.einsum('bhts,bshd->bthd', a.astype(v.dtype), v)

    out = lax.map(block, (split_blocks(q, nb), q_pos.reshape(nb, tb)))
    return merge_blocks(out)


def forgetting_attention(q, k, v, cum_q, cum_k, q_pos, k_pos):
    T = q.shape[1]
    tb = min(Q_BLOCK, T)
    nb = T // tb
    scale = DH ** -0.5
    cum_k_t = jnp.transpose(cum_k, (0, 2, 1))

    def block(args):
        qb, cqb, pb = args
        decay = jnp.transpose(cqb, (0, 2, 1))[..., None] - cum_k_t[:, :, None, :]
        s = jnp.einsum('bthd,bshd->bhts', qb, k).astype(jnp.float32) * scale + decay
        s = jnp.where(pb[:, None] >= k_pos[None, :], s, NEG_INF)
        pr = jax.nn.softmax(s, axis=-1)
        return jnp.einsum('bhts,bshd->bthd', pr.astype(v.dtype), v)

    out = lax.map(block, (split_blocks(q, nb), split_blocks(cum_q, nb), q_pos.reshape(nb, tb)))
    return merge_blocks(out)


def swiglu(h, wg, wu, wd):
    return (jax.nn.silu(h @ wg) * (h @ wu)) @ wd


def moe_swiglu(h, router, wg, wu, wd):
    logits = jnp.einsum('btd,de->bte', h, router).astype(jnp.float32)
    top_v, top_i = lax.top_k(logits, TOP_K)
    w = jax.nn.softmax(top_v, axis=-1)
    gate = jnp.einsum('btk,btke->bte', w,
                      jax.nn.one_hot(top_i, N_EXPERTS, dtype=jnp.float32)).astype(h.dtype)
    out = jnp.zeros_like(h)
    for e in range(N_EXPERTS):
        out = out + gate[..., e:e + 1] * swiglu(h, wg[e], wu[e], wd[e])
    return out


def mixer_sublayer(h, past, past_len, l, p):
    B, T, _ = h.shape
    proj = h @ p['w_in'][l]
    offs = np.cumsum(COL_SIZES)[:-1].tolist()
    qa, ka, va, qb, kb, vb, fb, ga, gb = jnp.split(proj, offs, axis=-1)
    qa = qa.reshape(B, T, HA, 2, DK)
    ka = ka.reshape(B, T, HA, 2, DK)
    va = va.reshape(B, T, HA, DVA)
    qb = qb.reshape(B, T, HB, DH)
    kb = kb.reshape(B, T, HB, DH)
    vb = vb.reshape(B, T, HB, DH)
    logf = jax.nn.log_sigmoid(fb.astype(jnp.float32) + p['b_fox_f'][l].astype(jnp.float32))
    if past is None:
        ka_all, va_all, kb_all, vb_all, logf_all = ka, va, kb, vb, logf
    else:
        pk_a, pv_a, pk_b, pv_b, plogf = past
        ka_all = jnp.concatenate([pk_a.reshape(B, -1, HA, 2, DK), ka], axis=1)
        va_all = jnp.concatenate([pv_a, va], axis=1)
        kb_all = jnp.concatenate([pk_b, kb], axis=1)
        vb_all = jnp.concatenate([pv_b, vb], axis=1)
        logf_all = jnp.concatenate([plogf.astype(jnp.float32), logf], axis=1)
    q_pos = past_len + jnp.arange(T, dtype=jnp.int32)
    k_pos = jnp.arange(past_len + T, dtype=jnp.int32)

    lam_init = 0.8 - 0.6 * math.exp(-0.3 * l)
    lam = (jnp.exp(jnp.sum(p['lambda_q1'][l].astype(jnp.float32) * p['lambda_k1'][l].astype(jnp.float32)))
           - jnp.exp(jnp.sum(p['lambda_q2'][l].astype(jnp.float32) * p['lambda_k2'][l].astype(jnp.float32)))
           + lam_init)
    ya = diff_attention(qa, ka_all, va_all, q_pos, k_pos, p['rel_bias'], lam)
    ya = rmsnorm(ya, p['subln_g'][l]) * (1.0 - lam_init)

    cum_k = jnp.cumsum(logf_all, axis=1)
    cum_q = cum_k[:, past_len:]
    yb = forgetting_attention(qb, kb_all, vb_all, cum_q, cum_k, q_pos, k_pos)

    ya = ya.reshape(B, T, HA * DVA) @ p['w_branch_a'][l]
    yb = yb.reshape(B, T, HB * DH) @ p['w_branch_b'][l]
    merged = jax.nn.sigmoid(ga) * ya + jax.nn.sigmoid(gb) * yb
    out = merged @ p['w_out'][l]
    new_rows = (ka.reshape(B, T, HA, 2 * DK), va, kb, vb, logf)
    return out, new_rows


def trunk(x, c, caches, page_table, past_len, p):
    rows = [[], [], [], [], []]
    for l in range(DEPTH):
        mod = (c @ p['w_ada'][l] + p['b_ada'][l])[:, None, :]
        sh1, sc1, g1, sh2, sc2, g2 = jnp.split(mod, 6, axis=-1)
        h = rmsnorm(x, p['norm_attn_g'][l]) * (1.0 + sc1) + sh1
        past = None if caches is None else tuple(gather_pages(cc[l], page_table) for cc in caches)
        out, new = mixer_sublayer(h, past, past_len, l, p)
        x = x + g1 * out
        h = rmsnorm(x, p['norm_ffn_g'][l]) * (1.0 + sc2) + sh2
        if l % 2 == 0:
            i = l // 2
            f = swiglu(h, p['ffn_w_gate'][i], p['ffn_w_up'][i], p['ffn_w_down'][i])
        else:
            i = l // 2
            f = moe_swiglu(h, p['moe_router'][i], p['moe_w_gate'][i], p['moe_w_up'][i], p['moe_w_down'][i])
        x = x + g2 * f
        for r, n in zip(rows, new):
            r.append(n)
    y = rmsnorm(x, p['final_norm_g'])
    return y, tuple(jnp.stack(r) for r in rows)


def setup_inputs(seed: int = 0) -> dict:
    key = jax.random.key(seed)
    ks = jax.random.split(key, 40)
    n_pages = PAST_LEN // PAGE_SIZE
    n_phys = (DEC_BATCH * n_pages * 5) // 4
    f32 = jnp.float32
    D = D_MODEL

    def nrm(k, shape, scale):
        return jax.random.normal(k, shape, f32) * scale

    page_table = jax.random.permutation(ks[9], n_phys)[: DEC_BATCH * n_pages].reshape(DEC_BATCH, n_pages).astype(jnp.int32)
    return {
        'x_prompt': nrm(ks[0], (BATCH, SEQ, D), 1.0),
        'x_sample': nrm(ks[1], (DEC_BATCH, DEC_SEQ, D), 1.0),
        'c_prompt': nrm(ks[2], (BATCH, D), 1.0),
        'c_sample': nrm(ks[3], (DEC_BATCH, D), 1.0),
        'cache_diff_k': nrm(ks[4], (DEPTH, n_phys, PAGE_SIZE, HA, 2 * DK), 1.0),
        'cache_diff_v': nrm(ks[5], (DEPTH, n_phys, PAGE_SIZE, HA, DVA), 1.0),
        'cache_fox_k': nrm(ks[6], (DEPTH, n_phys, PAGE_SIZE, HB, DH), 1.0),
        'cache_fox_v': nrm(ks[7], (DEPTH, n_phys, PAGE_SIZE, HB, DH), 1.0),
        'cache_fox_logf': jax.nn.log_sigmoid(2.0 + nrm(ks[8], (DEPTH, n_phys, PAGE_SIZE, HB), 1.0)),
        'page_table': page_table,
        'w_ada': nrm(ks[10], (DEPTH, D, 6 * D), 0.5 * D ** -0.5),
        'b_ada': nrm(ks[11], (DEPTH, 6 * D), 0.02),
        'norm_attn_g': 1.0 + nrm(ks[12], (DEPTH, D), 0.02),
        'norm_ffn_g': 1.0 + nrm(ks[13], (DEPTH, D), 0.02),
        'w_in': nrm(ks[14], (DEPTH, D, IN_COLS), D ** -0.5),
        'b_fox_f': 2.0 + nrm(ks[15], (DEPTH, HB), 0.5),
        'lambda_q1': nrm(ks[16], (DEPTH, DK), 0.1),
        'lambda_k1': nrm(ks[17], (DEPTH, DK), 0.1),
        'lambda_q2': nrm(ks[18], (DEPTH, DK), 0.1),
        'lambda_k2': nrm(ks[19], (DEPTH, DK), 0.1),
        'subln_g': 1.0 + nrm(ks[20], (DEPTH, DVA), 0.02),
        'rel_bias': nrm(ks[21], (NUM_BUCKETS, HA), 0.5),
        'w_branch_a': nrm(ks[22], (DEPTH, HA * DVA, D), (HA * DVA) ** -0.5),
        'w_branch_b': nrm(ks[23], (DEPTH, HB * DH, D), (HB * DH) ** -0.5),
        'w_out': nrm(ks[24], (DEPTH, D, D), D ** -0.5),
        'ffn_w_gate': nrm(ks[25], (N_DENSE, D, D_FF), D ** -0.5),
        'ffn_w_up': nrm(ks[26], (N_DENSE, D, D_FF), D ** -0.5),
        'ffn_w_down': nrm(ks[27], (N_DENSE, D_FF, D), D_FF ** -0.5),
        'moe_router': nrm(ks[28], (N_MOE, D, N_EXPERTS), D ** -0.5),
        'moe_w_gate': nrm(ks[29], (N_MOE, N_EXPERTS, D, D_FF), D ** -0.5),
        'moe_w_up': nrm(ks[30], (N_MOE, N_EXPERTS, D, D_FF), D ** -0.5),
        'moe_w_down': nrm(ks[31], (N_MOE, N_EXPERTS, D_FF, D), D_FF ** -0.5),
        'final_norm_g': 1.0 + nrm(ks[32], (D,), 0.02),
    }


def reference(x_prompt, x_sample, c_prompt, c_sample, cache_diff_k, cache_diff_v, cache_fox_k,
              cache_fox_v, cache_fox_logf, page_table, w_ada, b_ada, norm_attn_g, norm_ffn_g, w_in,
              b_fox_f, lambda_q1, lambda_k1, lambda_q2, lambda_k2, subln_g, rel_bias, w_branch_a,
              w_branch_b, w_out, ffn_w_gate, ffn_w_up, ffn_w_down, moe_router, moe_w_gate, moe_w_up,
              moe_w_down, final_norm_g):
    p = dict(w_ada=w_ada, b_ada=b_ada, norm_attn_g=norm_attn_g, norm_ffn_g=norm_ffn_g, w_in=w_in,
             b_fox_f=b_fox_f, lambda_q1=lambda_q1, lambda_k1=lambda_k1, lambda_q2=lambda_q2,
             lambda_k2=lambda_k2, subln_g=subln_g, rel_bias=rel_bias, w_branch_a=w_branch_a,
             w_branch_b=w_branch_b, w_out=w_out, ffn_w_gate=ffn_w_gate, ffn_w_up=ffn_w_up,
             ffn_w_down=ffn_w_down, moe_router=moe_router, moe_w_gate=moe_w_gate, moe_w_up=moe_w_up,
             moe_w_down=moe_w_down, final_norm_g=final_norm_g)
    past_len = page_table.shape[1] * cache_diff_k.shape[2]
    y_prompt, (p_dk, p_dv, p_fk, p_fv, p_lf) = trunk(x_prompt, c_prompt, None, None, 0, p)
    caches = (cache_diff_k, cache_diff_v, cache_fox_k, cache_fox_v, cache_fox_logf)
    y_sample, (s_dk, s_dv, s_fk, s_fv, s_lf) = trunk(x_sample, c_sample, caches, page_table, past_len, p)
    return (y_prompt, y_sample, p_dk, p_dv, p_fk, p_fv, p_lf, s_dk, s_dv, s_fk, s_fv, s_lf)
```

```python
import functools
import math

import jax
import jax.numpy as jnp
import numpy as np
from jax import lax
from jax.experimental import pallas as pl
from jax.experimental.pallas import tpu as pltpu

F32 = jnp.float32
BF16 = jnp.bfloat16

EPS = 1e-6
NEG_INF = -1e30
NUM_BUCKETS = 32
MAX_DISTANCE = 128
TOP_K = 2

LANES = 128
SUBLANES = 8
VMEM_LIMIT = 56 * 1024 * 1024


def _cparams(sem):
    return pltpu.CompilerParams(dimension_semantics=sem, vmem_limit_bytes=VMEM_LIMIT)


def _pick_tile(n, pref):
    t = min(n, pref)
    while n % t:
        t //= 2
    return t


def _t5_bucket_np(rel):
    n = np.maximum(rel, 0)
    max_exact = NUM_BUCKETS // 2
    nf = np.maximum(n, 1).astype(np.float32)
    large = max_exact + (np.log(nf / np.float32(max_exact)) / np.float32(math.log(MAX_DISTANCE / max_exact))
                         * np.float32(NUM_BUCKETS - max_exact)).astype(np.int32)
    large = np.minimum(large, NUM_BUCKETS - 1)
    return np.where(n < max_exact, n, large).astype(np.int32)


def _ada_kernel(cp_ref, cs_ref, w_ref, b_ref, op_ref, os_ref):
    w = w_ref[...].astype(BF16)
    b = b_ref[...]
    op_ref[...] = jnp.dot(cp_ref[...].astype(BF16), w, preferred_element_type=F32) + b
    os_ref[...] = jnp.dot(cs_ref[...].astype(BF16), w, preferred_element_type=F32) + b


def _ada_mod(c_p, c_s, w_ada, b_ada):
    L, D, D6 = w_ada.shape
    nj = D6 // D
    bp, bs = c_p.shape[0], c_s.shape[0]
    b3 = b_ada.reshape(L, 1, D6)
    return pl.pallas_call(
        _ada_kernel,
        grid=(L, nj),
        in_specs=[
            pl.BlockSpec((bp, D), lambda l, j: (0, 0)),
            pl.BlockSpec((bs, D), lambda l, j: (0, 0)),
            pl.BlockSpec((None, D, D), lambda l, j: (l, 0, j)),
            pl.BlockSpec((None, 1, D), lambda l, j: (l, 0, j)),
        ],
        out_specs=[
            pl.BlockSpec((None, bp, D), lambda l, j: (l * nj + j, 0, 0)),
            pl.BlockSpec((None, bs, D), lambda l, j: (l * nj + j, 0, 0)),
        ],
        out_shape=[jax.ShapeDtypeStruct((L * nj, bp, D), F32),
                   jax.ShapeDtypeStruct((L * nj, bs, D), F32)],
        compiler_params=_cparams(("arbitrary", "arbitrary")),
    )(c_p, c_s, w_ada, b3)


class _Mod:
    def __init__(self, arr, rows_per_mod_row, nb=1):
        self.arr = arr
        self.rows_per = rows_per_mod_row
        self.nb = nb

    def spec(self, l, j, tm):
        D = self.arr.shape[-1]
        idx = l * 6 + j
        if self.rows_per == 1:
            return pl.BlockSpec((None, tm, D), lambda i, *_: (idx, i, 0))
        assert self.rows_per % tm == 0
        per = self.rows_per // tm
        nb = self.nb
        return pl.BlockSpec((None, 1, D), lambda i, *_: (idx * nb + i // per, 0, 0))


def _rmsnorm_mod(x, g, sc, sh):
    y = x * lax.rsqrt(jnp.mean(x * x, axis=-1, keepdims=True) + EPS)
    return (y * g) * (1.0 + sc) + sh


def _inproj_kernel(x_ref, sc_ref, sh_ref, g_ref, wqkv_ref, wf_ref, wg_ref, bf_ref, *outs,
                   hw, n_f, emit_bf16, scale_a, scale_b):
    if emit_bf16:
        (qa_ref, ka_ref, va_ref, qb_ref, kb_ref, vb_ref, lf_ref, sga_ref, sgb_ref,
         kab_ref, vab_ref, kbb_ref, vbb_ref) = outs
    else:
        qa_ref, ka_ref, va_ref, qb_ref, kb_ref, vb_ref, lf_ref, sga_ref, sgb_ref = outs
        kab_ref = vab_ref = kbb_ref = vbb_ref = None
    h = _rmsnorm_mod(x_ref[...], g_ref[...], sc_ref[...], sh_ref[...]).astype(BF16)

    def proj(c):
        return jnp.dot(h, wqkv_ref[:, c * hw:(c + 1) * hw], preferred_element_type=F32)

    qa_ref[...] = (proj(0) * scale_a).astype(qa_ref.dtype)
    for c, (o32, o16) in zip((1, 2), ((ka_ref, kab_ref), (va_ref, vab_ref))):
        y = proj(c)
        o32[...] = y
        if o16 is not None:
            o16[...] = y.astype(BF16)
    qb_ref[...] = (proj(3) * scale_b).astype(qb_ref.dtype)
    for c, (o32, o16) in zip((4, 5), ((kb_ref, kbb_ref), (vb_ref, vbb_ref))):
        y = proj(c)
        o32[...] = y
        if o16 is not None:
            o16[...] = y.astype(BF16)
    z = jnp.dot(h, wf_ref[...], preferred_element_type=F32)[:, :n_f] + bf_ref[...]
    lf_ref[...] = jnp.minimum(z, 0.0) - jnp.log(1.0 + jnp.exp(-jnp.abs(z)))
    D = sga_ref.shape[-1]
    ga = jnp.dot(h, wg_ref[:, :D], preferred_element_type=F32)
    sga_ref[...] = jax.nn.sigmoid(ga).astype(sga_ref.dtype)
    gb = jnp.dot(h, wg_ref[:, D:], preferred_element_type=F32)
    sgb_ref[...] = jax.nn.sigmoid(gb).astype(sgb_ref.dtype)


def _inproj(x, mod, l, g_attn, wqkv, wf, wg, b_f, *, emit_bf16, scale_a, scale_b, tm):
    N, D = x.shape
    hw = wqkv.shape[1] // 6
    n_f = b_f.shape[-1]
    row = lambda w, dt: (pl.BlockSpec((tm, w), lambda i: (i, 0)), jax.ShapeDtypeStruct((N, w), dt))
    qdt = BF16 if emit_bf16 else F32
    outs = [row(hw, qdt), row(hw, F32), row(hw, F32), row(hw, qdt), row(hw, F32), row(hw, F32),
            row(n_f, F32), row(D, BF16), row(D, BF16)]
    if emit_bf16:
        outs += [row(hw, BF16)] * 4
    full = lambda a: pl.BlockSpec(a.shape, lambda i: (0,) * a.ndim)
    return pl.pallas_call(
        functools.partial(_inproj_kernel, hw=hw, n_f=n_f, emit_bf16=emit_bf16,
                          scale_a=scale_a, scale_b=scale_b),
        grid=(N // tm,),
        in_specs=[pl.BlockSpec((tm, D), lambda i: (i, 0)), mod.spec(l, 1, tm), mod.spec(l, 0, tm),
                  full(g_attn), full(wqkv), full(wf), full(wg), full(b_f)],
        out_specs=[o[0] for o in outs],
        out_shape=[o[1] for o in outs],
        compiler_params=_cparams(("arbitrary",)),
    )(x, mod.arr, mod.arr, g_attn, wqkv, wf, wg, b_f)


def _two_map_queries(q):
    lane = lax.broadcasted_iota(jnp.int32, q.shape, 1)
    half = q.shape[-1] // 2
    zero = jnp.zeros_like(q)
    return jnp.where(lane < half, q, zero), jnp.where(lane >= half, q, zero)


def _finish_diff(o1, o2, lam, out_scale, g):
    y = o1 - lam * o2
    y = y * lax.rsqrt(jnp.mean(y * y, axis=-1, keepdims=True) + EPS)
    return (y * g) * out_scale


def _finish_fox(o1, o2):
    lane = lax.broadcasted_iota(jnp.int32, o1.shape, 1)
    return jnp.where(lane < o1.shape[-1] // 2, o1, o2)


def _flash_prompt_kernel(scal_ref, q_ref, k_ref, v_ref, *rest, mode, t):
    if mode == "diff":
        b0_ref, b1_ref, g_ref, o_ref, m_sc, l_sc, acc_sc = rest
    else:
        cq_ref, ck_ref, o_ref, m_sc, l_sc, acc_sc = rest
    gi = pl.program_id(1)
    qi = pl.program_id(2)
    q1, q2 = _two_map_queries(q_ref[...])
    qs = (q1, q2)
    m_sc[...] = jnp.full(m_sc.shape, NEG_INF, F32)
    l_sc[...] = jnp.zeros(l_sc.shape, F32)
    acc_sc[...] = jnp.zeros(acc_sc.shape, F32)
    row = lax.broadcasted_iota(jnp.int32, (t, t), 0)
    col = lax.broadcasted_iota(jnp.int32, (t, t), 1)
    causal = row >= col
    if mode == "diff":
        c_far = scal_ref[gi]
    else:
        cqs = (cq_ref[:, 0:1], cq_ref[:, 1:2])

    def step(kj, kind):
        ks = pl.multiple_of(kj * t, t)
        k = k_ref[pl.ds(ks, t), :]
        v = v_ref[pl.ds(ks, t), :]
        for mi in range(2):
            s = lax.dot_general(qs[mi], k, (((1,), (1,)), ((), ())), preferred_element_type=F32)
            if mode == "diff":
                if kind == "far":
                    s = s + c_far
                elif kind == "near":
                    s = s + b1_ref[...]
                else:
                    s = jnp.where(causal, s + b0_ref[...], NEG_INF)
            else:
                s = s + (cqs[mi] - ck_ref[mi, pl.ds(kj, 1), :])
                if kind == "diag":
                    s = jnp.where(causal, s, NEG_INF)
            m_prev = m_sc[mi]
            m_new = jnp.maximum(m_prev, jnp.max(s, axis=1, keepdims=True))
            alpha = jnp.exp(m_prev - m_new)
            p = jnp.exp(s - m_new)
            l_sc[mi] = alpha * l_sc[mi] + jnp.sum(p, axis=1, keepdims=True)
            acc_sc[mi] = alpha * acc_sc[mi] + jnp.dot(p.astype(BF16), v, preferred_element_type=F32)
            m_sc[mi] = m_new

    def far_body(kj, c):
        step(kj, "far")
        return c

    if mode == "diff":
        lax.fori_loop(0, qi - 1, far_body, 0)

        @pl.when(qi >= 1)
        def _():
            step(qi - 1, "near")
    else:
        lax.fori_loop(0, qi, far_body, 0)
    step(qi, "diag")

    o1 = acc_sc[0] / l_sc[0]
    o2 = acc_sc[1] / l_sc[1]
    if mode == "diff":
        nh = scal_ref.shape[0] - 2
        y = _finish_diff(o1, o2, scal_ref[nh], scal_ref[nh + 1], g_ref[...])
    else:
        y = _finish_fox(o1, o2)
    o_ref[...] = y.astype(o_ref.dtype)


def _flash_prompt(mode, q, k, v, scal, extra, *, t):
    B, T, W = q.shape
    G = W // LANES
    nq = T // t
    qspec = pl.BlockSpec((None, t, LANES), lambda b, g, i: (b, i, g))
    kvspec = pl.BlockSpec((None, T, LANES), lambda b, g, i: (b, 0, g))
    if mode == "diff":
        b0, b1, sg = extra
        especs = [pl.BlockSpec((None, t, t), lambda b, g, i: (g, 0, 0)),
                  pl.BlockSpec((None, t, t), lambda b, g, i: (g, 0, 0)),
                  pl.BlockSpec((1, LANES), lambda b, g, i: (0, 0))]
    else:
        cq, ck = extra
        especs = [pl.BlockSpec((None, None, t, 2), lambda b, g, i: (b, g, i, 0)),
                  pl.BlockSpec((None, None, 2, nq, t), lambda b, g, i: (b, g, 0, 0, 0))]
    return pl.pallas_call(
        functools.partial(_flash_prompt_kernel, mode=mode, t=t),
        grid=(B, G, nq),
        in_specs=[pl.BlockSpec(memory_space=pltpu.SMEM), qspec, kvspec, kvspec] + especs,
        out_specs=qspec,
        out_shape=jax.ShapeDtypeStruct((B, T, W), BF16),
        scratch_shapes=[pltpu.VMEM((2, t, 1), F32), pltpu.VMEM((2, t, 1), F32),
                        pltpu.VMEM((2, t, LANES), F32)],
        compiler_params=_cparams(("arbitrary", "arbitrary", "arbitrary")),
    )(scal, q, k, v, *extra)


def _attn_sample_kernel(pt_ref, scal_ref, q_ref, kn_ref, vn_ref, *rest, mode, n_pages, page, n_groups):
    del pt_ref
    if mode == "diff":
        bias_ref, g_ref = rest[:2]
        rest = rest[2:]
    else:
        cq_ref, ck_ref = rest[:2]
        rest = rest[2:]
    k_refs = rest[:n_pages]
    v_refs = rest[n_pages:2 * n_pages]
    o_ref = rest[2 * n_pages]
    S = q_ref.shape[0]
    nt_dims = (((1,), (1,)), ((), ()))
    rows = 2 * S
    nblk = n_pages + 1
    r_i = lax.broadcasted_iota(jnp.int32, (rows, page), 0)
    c_i = lax.broadcasted_iota(jnp.int32, (rows, page), 1)
    tok = jnp.where(r_i >= S, r_i - S, r_i)
    new_ok = (c_i < S) & (c_i <= tok)
    top = lax.broadcasted_iota(jnp.int32, (rows, 1), 0) < S
    pad = jnp.zeros((page - S, LANES), F32)
    for g in range(n_groups):
        ls = slice(g * LANES, (g + 1) * LANES)
        q1, q2 = _two_map_queries(q_ref[:, ls])
        q2m = jnp.concatenate([q1, q2], axis=0).astype(BF16)
        kn = jnp.concatenate([kn_ref[:, ls], pad], axis=0).astype(BF16)
        vn = jnp.concatenate([vn_ref[:, ls], pad], axis=0).astype(BF16)
        s_blocks = []
        for p in range(nblk):
            if p == n_pages:
                s = lax.dot_general(q2m, kn, nt_dims, preferred_element_type=F32)
            elif mode == "diff":
                kp = k_refs[p][pl.ds(g, page, stride=n_groups), :].astype(BF16)
                s = lax.dot_general(q2m, kp, nt_dims, preferred_element_type=F32)
            else:
                kp = k_refs[p][g * LANES:(g + 1) * LANES, :].astype(BF16)
                s = jnp.dot(q2m, kp, preferred_element_type=F32)
            s_blocks.append(s)
        if mode == "diff":
            add = [bias_ref[g, :, p * page:(p + 1) * page] for p in range(nblk)]
        else:
            cq = jnp.where(top, jnp.concatenate([cq_ref[:, 2 * g:2 * g + 1]] * 2, axis=0),
                           jnp.concatenate([cq_ref[:, 2 * g + 1:2 * g + 2]] * 2, axis=0))
            add = []
            for p in range(nblk):
                cs = slice(p * page, (p + 1) * page)
                ck = jnp.where(top, ck_ref[2 * g:2 * g + 1, cs], ck_ref[2 * g + 1:2 * g + 2, cs])
                add.append(cq - ck)
        s_blocks = [s + a for s, a in zip(s_blocks, add)]
        s_blocks[n_pages] = jnp.where(new_ok, s_blocks[n_pages], NEG_INF)
        m = s_blocks[0].max(axis=1, keepdims=True)
        for s in s_blocks[1:]:
            m = jnp.maximum(m, s.max(axis=1, keepdims=True))
        l = jnp.zeros((rows, 1), F32)
        acc = jnp.zeros((rows, LANES), F32)
        for p in range(nblk):
            pr = jnp.exp(s_blocks[p] - m)
            l = l + pr.sum(axis=1, keepdims=True)
            prb = pr.astype(BF16)
            if p == n_pages:
                acc = acc + jnp.dot(prb, vn, preferred_element_type=F32)
            elif mode == "diff":
                vp = v_refs[p][pl.ds(g, page, stride=n_groups), :].astype(BF16)
                acc = acc + jnp.dot(prb, vp, preferred_element_type=F32)
            else:
                vp = v_refs[p][g * LANES:(g + 1) * LANES, :].astype(BF16)
                acc = acc + lax.dot_general(prb, vp, nt_dims, preferred_element_type=F32)
        o = acc / l
        o1, o2 = o[:S], o[S:]
        if mode == "diff":
            nh = scal_ref.shape[0] - 2
            y = _finish_diff(o1, o2, scal_ref[nh], scal_ref[nh + 1], g_ref[...])
        else:
            y = _finish_fox(o1, o2)
        o_ref[:, ls] = y.astype(o_ref.dtype)


def _attn_sample(mode, q, kn, vn, scal, extra, cache_k, cache_v, l, page_table):
    Bs, n_pages = page_table.shape
    W = q.shape[1]
    S = q.shape[0] // Bs
    n_groups = W // LANES
    page = cache_k.shape[2] // n_groups
    assert page == LANES
    rowspec = pl.BlockSpec((S, W), lambda b, pt: (b, 0))
    if mode == "diff":
        bias, sg = extra
        especs = [pl.BlockSpec(bias.shape, lambda b, pt: (0, 0, 0)),
                  pl.BlockSpec((1, LANES), lambda b, pt: (0, 0))]
    else:
        cq, ck = extra
        especs = [pl.BlockSpec((None,) + cq.shape[1:], lambda b, pt: (b, 0, 0)),
                  pl.BlockSpec((None,) + ck.shape[1:], lambda b, pt: (b, 0, 0))]
    pspecs = [pl.BlockSpec((None, None, page * n_groups, LANES), lambda b, pt, p=p: (l, pt[b, p], 0, 0))
              for p in range(n_pages)]
    grid_spec = pltpu.PrefetchScalarGridSpec(
        num_scalar_prefetch=1,
        grid=(Bs,),
        in_specs=[pl.BlockSpec(memory_space=pltpu.SMEM), rowspec, rowspec, rowspec] + especs + pspecs + pspecs,
        out_specs=rowspec,
    )
    return pl.pallas_call(
        functools.partial(_attn_sample_kernel, mode=mode, n_pages=n_pages, page=page, n_groups=n_groups),
        grid_spec=grid_spec,
        out_shape=jax.ShapeDtypeStruct(q.shape, F32),
        compiler_params=_cparams(("arbitrary",)),
    )(page_table, scal, q, kn, vn, *extra, *([cache_k] * n_pages), *([cache_v] * n_pages))


def _bias_kernel(tab_ref, bkt_ref, o_ref, *, n_buckets):
    h = pl.program_id(0)
    bkt = bkt_ref[...]
    acc = jnp.zeros(bkt.shape, F32)
    for b in range(n_buckets):
        acc = jnp.where(bkt == b, tab_ref[b, h], acc)
    o_ref[...] = acc


def _bias_lookup(rel_bias, buckets):
    nb, H = rel_bias.shape
    R, C = buckets.shape
    return pl.pallas_call(
        functools.partial(_bias_kernel, n_buckets=nb),
        grid=(H,),
        in_specs=[pl.BlockSpec(memory_space=pltpu.SMEM), pl.BlockSpec((R, C), lambda h: (0, 0))],
        out_specs=pl.BlockSpec((None, R, C), lambda h: (h, 0, 0)),
        out_shape=jax.ShapeDtypeStruct((H, R, C), F32),
        compiler_params=_cparams(("arbitrary",)),
    )(rel_bias, jnp.asarray(buckets))


def _post_attn_kernel(ya_ref, yb_ref, sga_ref, sgb_ref, x_ref, g1_ref, wa_ref, wb_ref, wo_ref, o_ref):
    a = jnp.dot(ya_ref[...].astype(BF16), wa_ref[...], preferred_element_type=F32)
    b = jnp.dot(yb_ref[...].astype(BF16), wb_ref[...], preferred_element_type=F32)
    merged = sga_ref[...].astype(F32) * a + sgb_ref[...].astype(F32) * b
    out = jnp.dot(merged.astype(BF16), wo_ref[...], preferred_element_type=F32)
    o_ref[...] = x_ref[...] + g1_ref[...] * out


def _post_attn(ya, yb, sga, sgb, x, mod, l, wa, wb, wo, *, tm):
    N, D = x.shape
    hw = ya.shape[1]
    full = lambda a: pl.BlockSpec(a.shape, lambda i: (0,) * a.ndim)
    r = lambda w: pl.BlockSpec((tm, w), lambda i: (i, 0))
    return pl.pallas_call(
        _post_attn_kernel,
        grid=(N // tm,),
        in_specs=[r(hw), r(hw), r(D), r(D), r(D), mod.spec(l, 2, tm), full(wa), full(wb), full(wo)],
        out_specs=r(D),
        out_shape=jax.ShapeDtypeStruct((N, D), F32),
        compiler_params=_cparams(("arbitrary",)),
    )(ya, yb, sga, sgb, x, mod.arr, wa, wb, wo)


def _ffn_norm_kernel(x_ref, sc_ref, sh_ref, g_ref, *rest, with_router):
    h = _rmsnorm_mod(x_ref[...], g_ref[...], sc_ref[...], sh_ref[...])
    hb = h.astype(BF16)
    if with_router:
        r_ref, h_ref, lg_ref = rest
        h_lo = (h - hb.astype(F32)).astype(BF16)
        r = r_ref[...]
        r_hi = r.astype(BF16)
        r_lo = (r - r_hi.astype(F32)).astype(BF16)
        lg_ref[...] = (jnp.dot(hb, r_hi, preferred_element_type=F32)
                       + jnp.dot(h_lo, r_hi, preferred_element_type=F32)
                       + jnp.dot(hb, r_lo, preferred_element_type=F32))
    else:
        (h_ref,) = rest
    h_ref[...] = hb


def _ffn_norm(x, mod, l, g_ffn, router_pad, *, tm):
    N, D = x.shape
    full = lambda a: pl.BlockSpec(a.shape, lambda i: (0,) * a.ndim)
    r = lambda w: pl.BlockSpec((tm, w), lambda i: (i, 0))
    ins = [r(D), mod.spec(l, 4, tm), mod.spec(l, 3, tm), full(g_ffn)]
    args = [x, mod.arr, mod.arr, g_ffn]
    outs = [(r(D), jax.ShapeDtypeStruct((N, D), BF16))]
    if router_pad is not None:
        ins.append(full(router_pad))
        args.append(router_pad)
        outs.append((r(LANES), jax.ShapeDtypeStruct((N, LANES), F32)))
    res = pl.pallas_call(
        functools.partial(_ffn_norm_kernel, with_router=router_pad is not None),
        grid=(N // tm,),
        in_specs=ins,
        out_specs=[o[0] for o in outs],
        out_shape=[o[1] for o in outs],
        compiler_params=_cparams(("arbitrary",)),
    )(*args)
    return res if router_pad is not None else (res[0], None)


def _ffn_kernel(te_ref, nt_ref, h_ref, wg_ref, wu_ref, wd_ref, o_ref, acc_ref):
    del te_ref
    i = pl.program_id(0)
    j = pl.program_id(1)
    nj = pl.num_programs(1)
    live = i < nt_ref[0]

    @pl.when(live)
    def _():
        h = h_ref[...]
        gt = jnp.dot(h, wg_ref[...], preferred_element_type=F32)
        up = jnp.dot(h, wu_ref[...], preferred_element_type=F32)
        a = (jax.nn.silu(gt) * up).astype(BF16)
        part = jnp.dot(a, wd_ref[...], preferred_element_type=F32)

        @pl.when(j == 0)
        def _():
            acc_ref[...] = part

        @pl.when(j > 0)
        def _():
            acc_ref[...] += part

        @pl.when(j == nj - 1)
        def _():
            o_ref[...] = acc_ref[...].astype(o_ref.dtype)

    @pl.when(jnp.logical_not(live) & (j == nj - 1))
    def _():
        o_ref[...] = jnp.zeros(o_ref.shape, o_ref.dtype)


def _ffn_grouped(h, tile_expert, n_tiles, wg, wu, wd, *, tm, tf):
    R, D = h.shape
    F = wg.shape[-1]
    nt_max = R // tm
    nj = F // tf

    def live_tile(i, nt):
        return jnp.minimum(i, jnp.maximum(nt[0] - 1, 0))

    grid_spec = pltpu.PrefetchScalarGridSpec(
        num_scalar_prefetch=2,
        grid=(nt_max, nj),
        in_specs=[
            pl.BlockSpec((tm, D), lambda i, j, te, nt: (live_tile(i, nt), 0)),
            pl.BlockSpec((None, D, tf), lambda i, j, te, nt: (te[live_tile(i, nt)], 0, j)),
            pl.BlockSpec((None, D, tf), lambda i, j, te, nt: (te[live_tile(i, nt)], 0, j)),
            pl.BlockSpec((None, tf, D), lambda i, j, te, nt: (te[live_tile(i, nt)], j, 0)),
        ],
        out_specs=pl.BlockSpec((tm, D), lambda i, j, te, nt: (i, 0)),
        scratch_shapes=[pltpu.VMEM((tm, D), F32)],
    )
    return pl.pallas_call(
        _ffn_kernel,
        grid_spec=grid_spec,
        out_shape=jax.ShapeDtypeStruct((R, D), BF16),
        compiler_params=_cparams(("arbitrary", "arbitrary")),
    )(tile_expert, n_tiles, h, wg, wu, wd)


def _combine_kernel(x_ref, g2_ref, *rest, n_terms, weighted, final):
    ys = rest[:n_terms]
    ws = rest[n_terms:2 * n_terms] if weighted else ()
    rest = rest[n_terms + len(ws):]
    if final:
        fg_ref, o_ref = rest
    else:
        (o_ref,) = rest
    f = None
    for k in range(n_terms):
        term = ys[k][...].astype(F32)
        if weighted:
            term = ws[k][...] * term
        f = term if f is None else f + term
    x = x_ref[...] + g2_ref[...] * f
    if final:
        x = (x * lax.rsqrt(jnp.mean(x * x, axis=-1, keepdims=True) + EPS)) * fg_ref[...]
    o_ref[...] = x


def _combine(x, mod, l, ys, ws, final_g, *, tm):
    N, D = x.shape
    r = lambda w: pl.BlockSpec((tm, w), lambda i: (i, 0))
    ins = [r(D), mod.spec(l, 5, tm)] + [r(D)] * len(ys) + [r(1)] * len(ws)
    args = [x, mod.arr, *ys, *ws]
    if final_g is not None:
        ins.append(pl.BlockSpec(final_g.shape, lambda i: (0, 0)))
        args.append(final_g)
    return pl.pallas_call(
        functools.partial(_combine_kernel, n_terms=len(ys), weighted=bool(ws), final=final_g is not None),
        grid=(N // tm,),
        in_specs=ins,
        out_specs=r(D),
        out_shape=jax.ShapeDtypeStruct((N, D), F32),
        compiler_params=_cparams(("arbitrary",)),
    )(*args)


def _route(logits, n_experts, tm):
    N = logits.shape[0]
    top_v, top_i = lax.top_k(logits, TOP_K)
    w = jax.nn.softmax(top_v, axis=-1)
    flat_e = top_i.reshape(-1).astype(jnp.int32)
    n_slots = flat_e.shape[0]
    order = jnp.argsort(flat_e, stable=True).astype(jnp.int32)
    sorted_e = flat_e[order]
    counts = jnp.sum(flat_e[:, None] == jnp.arange(n_experts, dtype=jnp.int32)[None, :], axis=0).astype(jnp.int32)
    padded = ((counts + tm - 1) // tm) * tm
    pad_end = jnp.cumsum(padded)
    pad_start = pad_end - padded
    start = jnp.cumsum(counts) - counts
    dest = pad_start[sorted_e] + (jnp.arange(n_slots, dtype=jnp.int32) - start[sorted_e])
    R = n_slots + n_experts * tm
    src_token = jnp.zeros((R,), jnp.int32).at[dest].set(order // TOP_K)
    pos = jnp.zeros((n_slots,), jnp.int32).at[order].set(dest).reshape(N, TOP_K)
    tile_ids = jnp.arange(R // tm, dtype=jnp.int32)
    tile_expert = jnp.minimum(jnp.searchsorted(pad_end // tm, tile_ids, side="right"), n_experts - 1).astype(jnp.int32)
    n_tiles = (pad_end[-1] // tm).reshape(1).astype(jnp.int32)
    return w, src_token, pos, tile_expert, n_tiles


def _prep_weights(p):
    HW = p["w_branch_a"].shape[1]
    w_in = p["w_in"]
    n_f = p["b_fox_f"].shape[-1]
    q_end = 6 * HW
    wf = jnp.pad(w_in[:, :, q_end:q_end + n_f], ((0, 0), (0, 0), (0, LANES - n_f)))
    router = p["moe_router"]
    return dict(
        wqkv=w_in[:, :, :q_end].astype(BF16),
        wf=wf.astype(BF16),
        wgate=w_in[:, :, q_end + n_f:].astype(BF16),
        wa=p["w_branch_a"].astype(BF16), wb=p["w_branch_b"].astype(BF16), wo=p["w_out"].astype(BF16),
        ffn_g=p["ffn_w_gate"].astype(BF16), ffn_u=p["ffn_w_up"].astype(BF16), ffn_d=p["ffn_w_down"].astype(BF16),
        moe_g=p["moe_w_gate"].astype(BF16), moe_u=p["moe_w_up"].astype(BF16), moe_d=p["moe_w_down"].astype(BF16),
        router=jnp.pad(router, ((0, 0), (0, 0), (0, LANES - router.shape[-1]))),
    )


def _trunk(x, mod, p, w, lam_scal, attn_fn, *, tm, tm_moe, tf, emit_bf16, dk, dh):
    depth = p["w_in"].shape[0]
    n_experts = p["moe_router"].shape[-1]
    rows = [[], [], [], [], []]
    for l in range(depth):
        pr = _inproj(x, mod, l, p["norm_attn_g"][l][None], w["wqkv"][l], w["wf"][l], w["wgate"][l],
                     p["b_fox_f"][l][None], emit_bf16=emit_bf16, scale_a=dk ** -0.5, scale_b=dh ** -0.5, tm=tm)
        qa, ka, va, qb, kb, vb, logf, sga, sgb = pr[:9]
        ya, yb = attn_fn(l, pr)
        x = _post_attn(ya, yb, sga, sgb, x, mod, l, w["wa"][l], w["wb"][l], w["wo"][l], tm=tm)
        final_g = p["final_norm_g"][None] if l == depth - 1 else None
        i = l // 2
        if l % 2 == 0:
            h, _ = _ffn_norm(x, mod, l, p["norm_ffn_g"][l][None], None, tm=tm)
            N = h.shape[0]
            tmd = _pick_tile(N, tm_moe)
            y = _ffn_grouped(h, jnp.zeros((N // tmd,), jnp.int32), jnp.full((1,), N // tmd, jnp.int32),
                             w["ffn_g"][i:i + 1], w["ffn_u"][i:i + 1], w["ffn_d"][i:i + 1], tm=tmd, tf=tf)
            x = _combine(x, mod, l, [y], [], final_g, tm=tm)
        else:
            h, logits = _ffn_norm(x, mod, l, p["norm_ffn_g"][l][None], w["router"][i], tm=tm)
            gate_w, src_token, pos, tile_expert, n_tiles = _route(logits[:, :n_experts], n_experts, tm_moe)
            y = _ffn_grouped(jnp.take(h, src_token, axis=0), tile_expert, n_tiles,
                             w["moe_g"][i], w["moe_u"][i], w["moe_d"][i], tm=tm_moe, tf=tf)
            ys = [jnp.take(y, pos[:, k], axis=0) for k in range(TOP_K)]
            ws = [gate_w[:, k:k + 1] for k in range(TOP_K)]
            x = _combine(x, mod, l, ys, ws, final_g, tm=tm)
        for r, n in zip(rows, (ka, va, kb, vb, logf)):
            r.append(n)
    return x, [jnp.stack(r) for r in rows]


def kernel(x_prompt, x_sample, c_prompt, c_sample, cache_diff_k, cache_diff_v, cache_fox_k, cache_fox_v, cache_fox_logf, page_table, w_ada, b_ada, norm_attn_g, norm_ffn_g, w_in, b_fox_f, lambda_q1, lambda_k1, lambda_q2, lambda_k2, subln_g, rel_bias, w_branch_a, w_branch_b, w_out, ffn_w_gate, ffn_w_up, ffn_w_down, moe_router, moe_w_gate, moe_w_up, moe_w_down, final_norm_g):
    p = dict(w_ada=w_ada, b_ada=b_ada, norm_attn_g=norm_attn_g, norm_ffn_g=norm_ffn_g, w_in=w_in,
             b_fox_f=b_fox_f, subln_g=subln_g, rel_bias=rel_bias, w_branch_a=w_branch_a,
             w_branch_b=w_branch_b, w_out=w_out, ffn_w_gate=ffn_w_gate, ffn_w_up=ffn_w_up,
             ffn_w_down=ffn_w_down, moe_router=moe_router, moe_w_gate=moe_w_gate, moe_w_up=moe_w_up,
             moe_w_down=moe_w_down, final_norm_g=final_norm_g)
    B, T, D = x_prompt.shape
    Bs, S, _ = x_sample.shape
    depth, n_phys, page, HA, DK2 = cache_diff_k.shape
    HB, DH = cache_fox_k.shape[3:]
    DK = DK2 // 2
    n_pages = page_table.shape[1]
    past_len = n_pages * page
    W = HA * DK2
    assert DK2 == LANES and 2 * DH == LANES and HB * DH == W and cache_diff_v.shape[-1] == LANES
    G = W // LANES
    d_ff = ffn_w_gate.shape[-1]
    tf = d_ff // 2 if (d_ff // 2) % LANES == 0 else d_ff

    w = _prep_weights(p)
    bp = B + (-B) % SUBLANES
    mod_p, mod_s = _ada_mod(jnp.pad(c_prompt, ((0, bp - B), (0, 0))), c_sample, w_ada, b_ada)
    mod_p = _Mod(mod_p.reshape(-1, 1, D), T, nb=bp)
    mod_s = _Mod(jnp.repeat(mod_s, S, axis=1), 1)

    lam_init = np.array([0.8 - 0.6 * math.exp(-0.3 * l) for l in range(depth)], np.float32)
    lam = (jnp.exp(jnp.sum(lambda_q1 * lambda_k1, axis=-1)) - jnp.exp(jnp.sum(lambda_q2 * lambda_k2, axis=-1))
           + lam_init)

    t = _pick_tile(T, 256)
    r_ = np.arange(t)[:, None]
    c_ = np.arange(t)[None, :]
    far_b = np.unique(_t5_bucket_np(np.arange(t + 1, max(T, t + 2))))
    assert far_b.size == 1, "far key blocks must share one bias bucket"
    bkt = np.concatenate([_t5_bucket_np(r_ - c_), _t5_bucket_np(t + r_ - c_)], axis=0)
    b01 = _bias_lookup(rel_bias, bkt)
    b0, b1 = b01[:, :t], b01[:, t:]
    c_far = rel_bias[int(far_b[0])]
    kpos = np.arange(past_len + page)[None, :]
    qpos = past_len + np.tile(np.arange(S), 2)[:, None]
    bias_s = _bias_lookup(rel_bias, _t5_bucket_np(qpos - kpos))

    def scal(l):
        return jnp.concatenate([c_far, lam[l][None], jnp.full((1,), 1.0 - float(lam_init[l]), F32)]).astype(F32)

    def attn_prompt(l, pr):
        qa, qb, logf = pr[0], pr[3], pr[6]
        kab, vab, kbb, vbb = pr[9:13]
        r3 = lambda a: a.reshape(B, T, W)
        ya = _flash_prompt("diff", r3(qa), r3(kab), r3(vab), scal(l), (b0, b1, subln_g[l][None]), t=t)
        cum = jnp.cumsum(logf.reshape(B, T, HB), axis=1)
        cq = cum.reshape(B, T, G, 2).transpose(0, 2, 1, 3)
        ck = cum.reshape(B, T, G, 2).transpose(0, 2, 3, 1).reshape(B, G, 2, T // t, t)
        yb = _flash_prompt("fox", r3(qb), r3(kbb), r3(vbb), jnp.zeros((1,), F32), (cq, ck), t=t)
        return ya.reshape(B * T, W), yb.reshape(B * T, W)

    tm_p = _pick_tile(T, 256)
    y_p, rows_p = _trunk(x_prompt.reshape(B * T, D), mod_p, p, w, lam, attn_prompt,
                         tm=tm_p, tm_moe=_pick_tile(T, 512), tf=tf, emit_bf16=True, dk=DK, dh=DH)

    diff_view = lambda c: c.reshape(depth, n_phys, page * HA, DK2)
    fox_view = lambda c: c.transpose(0, 1, 3, 4, 2).reshape(depth, n_phys, HB * DH, page)
    cdk, cdv, cfk, cfv = diff_view(cache_diff_k), diff_view(cache_diff_v), fox_view(cache_fox_k), fox_view(cache_fox_v)

    def attn_sample(l, pr):
        qa, ka, va, qb, kb, vb, logf = pr[:7]
        ya = _attn_sample("diff", qa, ka, va, scal(l), (bias_s, subln_g[l][None]), cdk, cdv, l, page_table)
        past_lf = cache_fox_logf[l][page_table].reshape(Bs, past_len, HB)
        cum = jnp.cumsum(jnp.concatenate([past_lf, logf.reshape(Bs, S, HB)], axis=1), axis=1)
        cq = cum[:, past_len:]
        ck = jnp.pad(cum, ((0, 0), (0, page - S), (0, 0))).transpose(0, 2, 1)
        yb = _attn_sample("fox", qb, kb, vb, jnp.zeros((1,), F32), (cq, ck), cfk, cfv, l, page_table)
        return ya, yb

    Ns = Bs * S
    y_s, rows_s = _trunk(x_sample.reshape(Ns, D), mod_s, p, w, lam, attn_sample,
                         tm=_pick_tile(Ns, 256), tm_moe=_pick_tile(Ns, 256), tf=tf, emit_bf16=False, dk=DK, dh=DH)

    def shape_rows(rows, b, s):
        dk_, dv_, fk_, fv_, lf_ = rows
        return (dk_.reshape(depth, b, s, HA, DK2), dv_.reshape(depth, b, s, HA, LANES),
                fk_.reshape(depth, b, s, HB, DH), fv_.reshape(depth, b, s, HB, DH), lf_.reshape(depth, b, s, HB))

    return (y_p.reshape(B, T, D), y_s.reshape(Bs, S, D)) + shape_rows(rows_p, B, T) + shape_rows(rows_s, Bs, S)
```

```python
import functools
import math

import jax
import jax.numpy as jnp
import numpy as np
from jax import lax
from jax.experimental import pallas as pl
from jax.experimental.pallas import tpu as pltpu

F32 = jnp.float32
BF16 = jnp.bfloat16

EPS = 1e-6
NEG_INF = -1e30
NUM_BUCKETS = 32
MAX_DISTANCE = 128
TOP_K = 2
LOG2E = 1.4426950408889634

LANES = 128
SUBLANES = 8
VMEM_LIMIT = 56 * 1024 * 1024


def _cparams(sem):
    return pltpu.CompilerParams(dimension_semantics=sem, vmem_limit_bytes=VMEM_LIMIT)


def _pick_tile(n, pref):
    t = min(n, pref)
    while n % t:
        t //= 2
    return t


def _t5_bucket_np(rel):
    n = np.maximum(rel, 0)
    max_exact = NUM_BUCKETS // 2
    nf = np.maximum(n, 1).astype(np.float32)
    large = max_exact + (np.log(nf / np.float32(max_exact)) / np.float32(math.log(MAX_DISTANCE / max_exact))
                         * np.float32(NUM_BUCKETS - max_exact)).astype(np.int32)
    large = np.minimum(large, NUM_BUCKETS - 1)
    return np.where(n < max_exact, n, large).astype(np.int32)


def _ada_kernel(cp_ref, cs_ref, w_ref, b_ref, op_ref, os_ref):
    w = w_ref[...].astype(BF16)
    b = b_ref[...]
    op_ref[...] = jnp.dot(cp_ref[...].astype(BF16), w, preferred_element_type=F32) + b
    os_ref[...] = jnp.dot(cs_ref[...].astype(BF16), w, preferred_element_type=F32) + b


def _ada_mod(c_p, c_s, w_ada, b_ada):
    L, D, D6 = w_ada.shape
    nj = D6 // D
    bp, bs = c_p.shape[0], c_s.shape[0]
    b3 = b_ada.reshape(L, 1, D6)
    return pl.pallas_call(
        _ada_kernel,
        grid=(L, nj),
        in_specs=[
            pl.BlockSpec((bp, D), lambda l, j: (0, 0)),
            pl.BlockSpec((bs, D), lambda l, j: (0, 0)),
            pl.BlockSpec((None, D, D), lambda l, j: (l, 0, j)),
            pl.BlockSpec((None, 1, D), lambda l, j: (l, 0, j)),
        ],
        out_specs=[
            pl.BlockSpec((None, bp, D), lambda l, j: (l * nj + j, 0, 0)),
            pl.BlockSpec((None, bs, D), lambda l, j: (l * nj + j, 0, 0)),
        ],
        out_shape=[jax.ShapeDtypeStruct((L * nj, bp, D), F32),
                   jax.ShapeDtypeStruct((L * nj, bs, D), F32)],
        compiler_params=_cparams(("arbitrary", "arbitrary")),
    )(c_p, c_s, w_ada, b3)


class _Mod:
    def __init__(self, arr, rows_per_mod_row, nb=1):
        self.arr = arr
        self.rows_per = rows_per_mod_row
        self.nb = nb

    def spec(self, l, j, tm):
        D = self.arr.shape[-1]
        idx = l * 6 + j
        if self.rows_per == 1:
            return pl.BlockSpec((None, tm, D), lambda i, *_: (idx, i, 0))
        assert self.rows_per % tm == 0
        per = self.rows_per // tm
        nb = self.nb
        return pl.BlockSpec((None, 1, D), lambda i, *_: (idx * nb + i // per, 0, 0))


def _rmsnorm_mod(x, g, sc, sh):
    y = x * lax.rsqrt(jnp.mean(x * x, axis=-1, keepdims=True) + EPS)
    return (y * g) * (1.0 + sc) + sh


def _inproj_kernel(x_ref, sc_ref, sh_ref, g_ref, wqkv_ref, wf_ref, wg_ref, bf_ref, *outs,
                   hw, n_f, emit_bf16, scale_a, scale_b):
    if emit_bf16:
        (qa_ref, ka_ref, va_ref, qb_ref, kb_ref, vb_ref, lf_ref, sga_ref, sgb_ref,
         kab_ref, vab_ref, kbb_ref, vbb_ref) = outs
    else:
        qa_ref, ka_ref, va_ref, qb_ref, kb_ref, vb_ref, lf_ref, sga_ref, sgb_ref = outs
        kab_ref = vab_ref = kbb_ref = vbb_ref = None
    h = _rmsnorm_mod(x_ref[...], g_ref[...], sc_ref[...], sh_ref[...]).astype(BF16)

    def proj(c):
        return jnp.dot(h, wqkv_ref[:, c * hw:(c + 1) * hw], preferred_element_type=F32)

    for c, o, scale in ((0, qa_ref, scale_a), (3, qb_ref, scale_b)):
        y = proj(c) * scale
        o[...] = (y.T if emit_bf16 else y).astype(o.dtype)
    for c, o32, o16, transposed in ((1, ka_ref, kab_ref, False), (2, va_ref, vab_ref, True),
                                    (4, kb_ref, kbb_ref, False), (5, vb_ref, vbb_ref, True)):
        y = proj(c)
        o32[...] = y
        if o16 is not None:
            o16[...] = (y.T if transposed else y).astype(BF16)
    z = jnp.dot(h, wf_ref[...], preferred_element_type=F32)[:, :n_f] + bf_ref[...]
    lf_ref[...] = jnp.minimum(z, 0.0) - jnp.log(1.0 + jnp.exp(-jnp.abs(z)))
    D = sga_ref.shape[-1]
    ga = jnp.dot(h, wg_ref[:, :D], preferred_element_type=F32)
    sga_ref[...] = jax.nn.sigmoid(ga).astype(sga_ref.dtype)
    gb = jnp.dot(h, wg_ref[:, D:], preferred_element_type=F32)
    sgb_ref[...] = jax.nn.sigmoid(gb).astype(sgb_ref.dtype)


def _inproj(x, mod, l, g_attn, wqkv, wf, wg, b_f, *, emit_bf16, scale_a, scale_b, tm):
    N, D = x.shape
    hw = wqkv.shape[1] // 6
    n_f = b_f.shape[-1]
    row = lambda w, dt: (pl.BlockSpec((tm, w), lambda i: (i, 0)), jax.ShapeDtypeStruct((N, w), dt))
    vt = (pl.BlockSpec((None, hw, tm), lambda i: (i, 0, 0)), jax.ShapeDtypeStruct((N // tm, hw, tm), BF16))
    qo = vt if emit_bf16 else row(hw, F32)
    outs = [qo, row(hw, F32), row(hw, F32), qo, row(hw, F32), row(hw, F32),
            row(n_f, F32), row(D, BF16), row(D, BF16)]
    if emit_bf16:
        outs += [row(hw, BF16), vt, row(hw, BF16), vt]
    full = lambda a: pl.BlockSpec(a.shape, lambda i: (0,) * a.ndim)
    return pl.pallas_call(
        functools.partial(_inproj_kernel, hw=hw, n_f=n_f, emit_bf16=emit_bf16,
                          scale_a=scale_a, scale_b=scale_b),
        grid=(N // tm,),
        in_specs=[pl.BlockSpec((tm, D), lambda i: (i, 0)), mod.spec(l, 1, tm), mod.spec(l, 0, tm),
                  full(g_attn), full(wqkv), full(wf), full(wg), full(b_f)],
        out_specs=[o[0] for o in outs],
        out_shape=[o[1] for o in outs],
        compiler_params=_cparams(("arbitrary",)),
    )(x, mod.arr, mod.arr, g_attn, wqkv, wf, wg, b_f)


def _two_map_queries(q):
    lane = lax.broadcasted_iota(jnp.int32, q.shape, 1)
    half = q.shape[-1] // 2
    zero = jnp.zeros_like(q)
    return jnp.where(lane < half, q, zero), jnp.where(lane >= half, q, zero)


def _finish_diff(o1, o2, lam, out_scale, g):
    y = o1 - lam * o2
    y = y * lax.rsqrt(jnp.mean(y * y, axis=-1, keepdims=True) + EPS)
    return (y * g) * out_scale


def _finish_fox(o1, o2):
    lane = lax.broadcasted_iota(jnp.int32, o1.shape, 1)
    return jnp.where(lane < o1.shape[-1] // 2, o1, o2)


def _split3_bf16(c):
    def trunc(x):
        bits = lax.bitcast_convert_type(x, jnp.uint32) & jnp.uint32(0xFFFF0000)
        return lax.bitcast_convert_type(bits, F32)
    hi = trunc(c)
    r = c - hi
    mid = trunc(r)
    return hi.astype(BF16), mid.astype(BF16), (r - mid).astype(BF16)


def _decay_columns(cum):
    B, T, G, _ = cum.shape
    parts = jnp.stack(_split3_bf16(cum), axis=-1)
    one = jnp.ones((B, T, G, 3), BF16)
    zero = jnp.zeros((B, T, G, 6), BF16)
    kx = jnp.concatenate([-parts[..., 0, :], one, -parts[..., 1, :], one], axis=-1)
    q0 = jnp.concatenate([one, parts[..., 0, :], zero], axis=-1)
    q1 = jnp.concatenate([zero, one, parts[..., 1, :]], axis=-1)
    padl = lambda a: jnp.pad(a, [(0, 0)] * (a.ndim - 1) + [(0, LANES - a.shape[-1])])
    kx = padl(kx).transpose(0, 2, 1, 3)
    qx = jnp.stack([padl(q0), padl(q1)], axis=0).transpose(1, 3, 0, 4, 2)
    return kx, qx


def _flash_prompt_kernel(scal_ref, q_ref, k_ref, vt_ref, *rest, mode, t):
    if mode == "diff":
        b0_ref, b1_ref, g_ref = rest[:3]
    else:
        kx_ref, qx_ref = rest[:2]
    o_ref, m_sc, l_sc, acc_sc, sa_sc, sb_sc, pb_sc, ab_sc = rest[-8:]
    qi = pl.program_id(2)
    qt = q_ref[...]
    feat = lax.broadcasted_iota(jnp.int32, qt.shape, 0)
    zero = jnp.zeros_like(qt)
    qq = jnp.concatenate([jnp.where(feat < LANES // 2, qt, zero),
                          jnp.where(feat >= LANES // 2, qt, zero)], axis=1)
    if mode == "fox":
        qq = jnp.concatenate([qq, jnp.concatenate([qx_ref[0], qx_ref[1]], axis=1)], axis=0)
    m_sc[...] = jnp.full(m_sc.shape, NEG_INF, F32)
    l_sc[...] = jnp.zeros(l_sc.shape, F32)
    acc_sc[...] = jnp.zeros(acc_sc.shape, F32)

    def scores(kj, kind):
        ks = pl.multiple_of(kj * t, t)
        k = k_ref[pl.ds(ks, t), :]
        if mode == "fox":
            k = jnp.concatenate([k, kx_ref[pl.ds(ks, t), :]], axis=1)
        s = jnp.dot(k, qq, preferred_element_type=F32)
        if kind == "diag":
            key = lax.broadcasted_iota(jnp.int32, (t, 2 * t), 0)
            qry = lax.broadcasted_iota(jnp.int32, (t, 2 * t), 1)
            qry = jnp.where(qry >= t, qry - t, qry)
            visible = key <= qry
        if mode == "diff":
            if kind == "near":
                s = s + b1_ref[...]
            elif kind == "diag":
                s = jnp.where(visible, s + b0_ref[...], NEG_INF)
        elif kind == "diag":
            s = jnp.where(visible, s, NEG_INF)
        return s

    def softmax_update(s):
        m_prev = m_sc[...]
        m_new = jnp.maximum(m_prev, jnp.max(s, axis=0, keepdims=True))
        alpha = jnp.exp2(m_prev - m_new)
        p = jnp.exp2(s - m_new)
        l_sc[...] = alpha * l_sc[...] + jnp.sum(p, axis=0, keepdims=True)
        m_sc[...] = m_new
        return p.astype(BF16), alpha

    def pv_update(kj, p, alpha):
        acc_sc[...] = alpha * acc_sc[...] + jnp.dot(vt_ref[kj], p, preferred_element_type=F32)

    def step(kj, kind):
        p, alpha = softmax_update(scores(kj, kind))
        pv_update(kj, p, alpha)

    n_far = jnp.maximum(qi - 1 if mode == "diff" else qi, 0)
    n_pairs = lax.shift_right_logical(n_far, 1)
    pb_sc[...] = jnp.zeros(pb_sc.shape, BF16)
    ab_sc[...] = jnp.ones(ab_sc.shape, F32)

    @pl.when(n_pairs > 0)
    def _():
        sa_sc[...] = scores(0, "far")

    def pair_body(i, c):
        a = 2 * i
        sb_sc[...] = scores(a + 1, "far")
        pv_update(jnp.maximum(a - 1, 0), pb_sc[...], ab_sc[...])
        p_a, alpha_a = softmax_update(sa_sc[...])
        sa_sc[...] = scores(jnp.minimum(a + 2, 2 * (n_pairs - 1)), "far")
        p_b, alpha_b = softmax_update(sb_sc[...])
        pv_update(a, p_a, alpha_a)
        pb_sc[...] = p_b
        ab_sc[...] = alpha_b
        return c

    lax.fori_loop(0, n_pairs, pair_body, 0)
    pv_update(jnp.maximum(2 * n_pairs - 1, 0), pb_sc[...], ab_sc[...])

    @pl.when(n_far - 2 * n_pairs == 1)
    def _():
        step(n_far - 1, "far")

    if mode == "diff":
        @pl.when(qi >= 1)
        def _():
            step(qi - 1, "near")
    step(qi, "diag")

    o = acc_sc[...] / l_sc[...]
    o1, o2 = o[:, :t], o[:, t:]
    if mode == "diff":
        nh = scal_ref.shape[0] - 2
        y = o1 - scal_ref[nh] * o2
        y = y * lax.rsqrt(jnp.mean(y * y, axis=0, keepdims=True) + EPS)
        y = (y * g_ref[...]) * scal_ref[nh + 1]
    else:
        feat = lax.broadcasted_iota(jnp.int32, o1.shape, 0)
        y = jnp.where(feat < LANES // 2, o1, o2)
    o_ref[...] = y.T.astype(o_ref.dtype)


def _flash_prompt(mode, qt, k, vt, scal, extra, *, t):
    B, T, W = k.shape
    G = W // LANES
    nq = T // t
    qspec = pl.BlockSpec((None, None, LANES, t), lambda b, g, i: (b, i, g, 0))
    ospec = pl.BlockSpec((None, t, LANES), lambda b, g, i: (b, i, g))
    kspec = pl.BlockSpec((None, T, LANES), lambda b, g, i: (b, 0, g))
    vspec = pl.BlockSpec((None, nq, LANES, t), lambda b, g, i: (b, 0, g, 0))
    if mode == "diff":
        especs = [pl.BlockSpec((None, t, 2 * t), lambda b, g, i: (g, 0, 0)),
                  pl.BlockSpec((None, t, 2 * t), lambda b, g, i: (g, 0, 0)),
                  pl.BlockSpec((LANES, 1), lambda b, g, i: (0, 0))]
    else:
        especs = [pl.BlockSpec((None, None, T, LANES), lambda b, g, i: (b, g, 0, 0)),
                  pl.BlockSpec((None, None, 2, LANES, t), lambda b, g, i: (b, g, 0, 0, i))]
    return pl.pallas_call(
        functools.partial(_flash_prompt_kernel, mode=mode, t=t),
        grid=(B, G, nq),
        in_specs=[pl.BlockSpec(memory_space=pltpu.SMEM), qspec, kspec, vspec] + especs,
        out_specs=ospec,
        out_shape=jax.ShapeDtypeStruct((B, T, W), BF16),
        scratch_shapes=[pltpu.VMEM((1, 2 * t), F32), pltpu.VMEM((1, 2 * t), F32),
                        pltpu.VMEM((LANES, 2 * t), F32),
                        pltpu.VMEM((t, 2 * t), F32), pltpu.VMEM((t, 2 * t), F32),
                        pltpu.VMEM((t, 2 * t), BF16), pltpu.VMEM((1, 2 * t), F32)],
        compiler_params=_cparams(("arbitrary", "arbitrary", "arbitrary")),
    )(scal, qt, k, vt, *extra)


def _attn_sample_kernel(pt_ref, scal_ref, q_ref, kn_ref, vn_ref, *rest, mode, n_pages, page, n_groups):
    del pt_ref
    if mode == "diff":
        bias_ref, g_ref = rest[:2]
        rest = rest[2:]
    else:
        cq_ref, ck_ref = rest[:2]
        rest = rest[2:]
    k_refs = rest[:n_pages]
    v_refs = rest[n_pages:2 * n_pages]
    o_ref = rest[2 * n_pages]
    S = q_ref.shape[0]
    nt_dims = (((1,), (1,)), ((), ()))
    rows = 2 * S
    nblk = n_pages + 1
    r_i = lax.broadcasted_iota(jnp.int32, (rows, page), 0)
    c_i = lax.broadcasted_iota(jnp.int32, (rows, page), 1)
    tok = jnp.where(r_i >= S, r_i - S, r_i)
    new_ok = (c_i < S) & (c_i <= tok)
    top = lax.broadcasted_iota(jnp.int32, (rows, 1), 0) < S
    pad = jnp.zeros((page - S, LANES), F32)
    for g in range(n_groups):
        ls = slice(g * LANES, (g + 1) * LANES)
        q1, q2 = _two_map_queries(q_ref[:, ls])
        q2m = jnp.concatenate([q1, q2], axis=0).astype(BF16)
        kn = jnp.concatenate([kn_ref[:, ls], pad], axis=0).astype(BF16)
        vn = jnp.concatenate([vn_ref[:, ls], pad], axis=0).astype(BF16)
        s_blocks = []
        for p in range(nblk):
            if p == n_pages:
                s = lax.dot_general(q2m, kn, nt_dims, preferred_element_type=F32)
            elif mode == "diff":
                kp = k_refs[p][pl.ds(g, page, stride=n_groups), :].astype(BF16)
                s = lax.dot_general(q2m, kp, nt_dims, preferred_element_type=F32)
            else:
                kp = k_refs[p][g * LANES:(g + 1) * LANES, :].astype(BF16)
                s = jnp.dot(q2m, kp, preferred_element_type=F32)
            s_blocks.append(s)
        if mode == "diff":
            add = [bias_ref[g, :, p * page:(p + 1) * page] for p in range(nblk)]
        else:
            cq = jnp.where(top, jnp.concatenate([cq_ref[:, 2 * g:2 * g + 1]] * 2, axis=0),
                           jnp.concatenate([cq_ref[:, 2 * g + 1:2 * g + 2]] * 2, axis=0))
            add = []
            for p in range(nblk):
                cs = slice(p * page, (p + 1) * page)
                ck = jnp.where(top, ck_ref[2 * g:2 * g + 1, cs], ck_ref[2 * g + 1:2 * g + 2, cs])
                add.append(cq - ck)
        s_blocks = [s + a for s, a in zip(s_blocks, add)]
        s_blocks[n_pages] = jnp.where(new_ok, s_blocks[n_pages], NEG_INF)
        m = s_blocks[0].max(axis=1, keepdims=True)
        for s in s_blocks[1:]:
            m = jnp.maximum(m, s.max(axis=1, keepdims=True))
        l = jnp.zeros((rows, 1), F32)
        acc = jnp.zeros((rows, LANES), F32)
        for p in range(nblk):
            pr = jnp.exp(s_blocks[p] - m)
            l = l + pr.sum(axis=1, keepdims=True)
            prb = pr.astype(BF16)
            if p == n_pages:
                acc = acc + jnp.dot(prb, vn, preferred_element_type=F32)
            elif mode == "diff":
                vp = v_refs[p][pl.ds(g, page, stride=n_groups), :].astype(BF16)
                acc = acc + jnp.dot(prb, vp, preferred_element_type=F32)
            else:
                vp = v_refs[p][g * LANES:(g + 1) * LANES, :].astype(BF16)
                acc = acc + lax.dot_general(prb, vp, nt_dims, preferred_element_type=F32)
        o = acc / l
        o1, o2 = o[:S], o[S:]
        if mode == "diff":
            nh = scal_ref.shape[0] - 2
            y = _finish_diff(o1, o2, scal_ref[nh], scal_ref[nh + 1], g_ref[...])
        else:
            y = _finish_fox(o1, o2)
        o_ref[:, ls] = y.astype(o_ref.dtype)


def _attn_sample(mode, q, kn, vn, scal, extra, cache_k, cache_v, l, page_table):
    Bs, n_pages = page_table.shape
    W = q.shape[1]
    S = q.shape[0] // Bs
    n_groups = W // LANES
    page = cache_k.shape[2] // n_groups
    assert page == LANES
    rowspec = pl.BlockSpec((S, W), lambda b, pt: (b, 0))
    if mode == "diff":
        bias, sg = extra
        especs = [pl.BlockSpec(bias.shape, lambda b, pt: (0, 0, 0)),
                  pl.BlockSpec((1, LANES), lambda b, pt: (0, 0))]
    else:
        cq, ck = extra
        especs = [pl.BlockSpec((None,) + cq.shape[1:], lambda b, pt: (b, 0, 0)),
                  pl.BlockSpec((None,) + ck.shape[1:], lambda b, pt: (b, 0, 0))]
    pspecs = [pl.BlockSpec((None, None, page * n_groups, LANES), lambda b, pt, p=p: (l, pt[b, p], 0, 0))
              for p in range(n_pages)]
    grid_spec = pltpu.PrefetchScalarGridSpec(
        num_scalar_prefetch=1,
        grid=(Bs,),
        in_specs=[pl.BlockSpec(memory_space=pltpu.SMEM), rowspec, rowspec, rowspec] + especs + pspecs + pspecs,
        out_specs=rowspec,
    )
    return pl.pallas_call(
        functools.partial(_attn_sample_kernel, mode=mode, n_pages=n_pages, page=page, n_groups=n_groups),
        grid_spec=grid_spec,
        out_shape=jax.ShapeDtypeStruct(q.shape, F32),
        compiler_params=_cparams(("arbitrary",)),
    )(page_table, scal, q, kn, vn, *extra, *([cache_k] * n_pages), *([cache_v] * n_pages))


def _bias_kernel(tab_ref, bkt_ref, o_ref, *, n_buckets):
    h = pl.program_id(0)
    bkt = bkt_ref[...]
    acc = jnp.zeros(bkt.shape, F32)
    for b in range(n_buckets):
        acc = jnp.where(bkt == b, tab_ref[b, h], acc)
    o_ref[...] = acc


def _bias_lookup(rel_bias, buckets):
    nb, H = rel_bias.shape
    R, C = buckets.shape
    return pl.pallas_call(
        functools.partial(_bias_kernel, n_buckets=nb),
        grid=(H,),
        in_specs=[pl.BlockSpec(memory_space=pltpu.SMEM), pl.BlockSpec((R, C), lambda h: (0, 0))],
        out_specs=pl.BlockSpec((None, R, C), lambda h: (h, 0, 0)),
        out_shape=jax.ShapeDtypeStruct((H, R, C), F32),
        compiler_params=_cparams(("arbitrary",)),
    )(rel_bias, jnp.asarray(buckets))


def _post_attn_kernel(ya_ref, yb_ref, sga_ref, sgb_ref, x_ref, g1_ref, wa_ref, wb_ref, wo_ref, o_ref):
    a = jnp.dot(ya_ref[...].astype(BF16), wa_ref[...], preferred_element_type=F32)
    b = jnp.dot(yb_ref[...].astype(BF16), wb_ref[...], preferred_element_type=F32)
    merged = sga_ref[...].astype(F32) * a + sgb_ref[...].astype(F32) * b
    out = jnp.dot(merged.astype(BF16), wo_ref[...], preferred_element_type=F32)
    o_ref[...] = x_ref[...] + g1_ref[...] * out


def _post_attn(ya, yb, sga, sgb, x, mod, l, wa, wb, wo, *, tm):
    N, D = x.shape
    hw = ya.shape[1]
    full = lambda a: pl.BlockSpec(a.shape, lambda i: (0,) * a.ndim)
    r = lambda w: pl.BlockSpec((tm, w), lambda i: (i, 0))
    return pl.pallas_call(
        _post_attn_kernel,
        grid=(N // tm,),
        in_specs=[r(hw), r(hw), r(D), r(D), r(D), mod.spec(l, 2, tm), full(wa), full(wb), full(wo)],
        out_specs=r(D),
        out_shape=jax.ShapeDtypeStruct((N, D), F32),
        compiler_params=_cparams(("arbitrary",)),
    )(ya, yb, sga, sgb, x, mod.arr, wa, wb, wo)


def _ffn_norm_kernel(x_ref, sc_ref, sh_ref, g_ref, *rest, with_router):
    h = _rmsnorm_mod(x_ref[...], g_ref[...], sc_ref[...], sh_ref[...])
    hb = h.astype(BF16)
    if with_router:
        r_ref, h_ref, lg_ref = rest
        h_lo = (h - hb.astype(F32)).astype(BF16)
        r = r_ref[...]
        r_hi = r.astype(BF16)
        r_lo = (r - r_hi.astype(F32)).astype(BF16)
        lg_ref[...] = (jnp.dot(hb, r_hi, preferred_element_type=F32)
                       + jnp.dot(h_lo, r_hi, preferred_element_type=F32)
                       + jnp.dot(hb, r_lo, preferred_element_type=F32))
    else:
        (h_ref,) = rest
    h_ref[...] = hb


def _ffn_norm(x, mod, l, g_ffn, router_pad, *, tm):
    N, D = x.shape
    full = lambda a: pl.BlockSpec(a.shape, lambda i: (0,) * a.ndim)
    r = lambda w: pl.BlockSpec((tm, w), lambda i: (i, 0))
    ins = [r(D), mod.spec(l, 4, tm), mod.spec(l, 3, tm), full(g_ffn)]
    args = [x, mod.arr, mod.arr, g_ffn]
    outs = [(r(D), jax.ShapeDtypeStruct((N, D), BF16))]
    if router_pad is not None:
        ins.append(full(router_pad))
        args.append(router_pad)
        outs.append((r(LANES), jax.ShapeDtypeStruct((N, LANES), F32)))
    res = pl.pallas_call(
        functools.partial(_ffn_norm_kernel, with_router=router_pad is not None),
        grid=(N // tm,),
        in_specs=ins,
        out_specs=[o[0] for o in outs],
        out_shape=[o[1] for o in outs],
        compiler_params=_cparams(("arbitrary",)),
    )(*args)
    return res if router_pad is not None else (res[0], None)


def _ffn_kernel(te_ref, nt_ref, h_ref, wg_ref, wu_ref, wd_ref, o_ref, acc_ref):
    del te_ref
    i = pl.program_id(0)
    j = pl.program_id(1)
    nj = pl.num_programs(1)
    live = i < nt_ref[0]

    @pl.when(live)
    def _():
        h = h_ref[...]
        gt = jnp.dot(h, wg_ref[...], preferred_element_type=F32)
        up = jnp.dot(h, wu_ref[...], preferred_element_type=F32)
        a = (jax.nn.silu(gt) * up).astype(BF16)
        part = jnp.dot(a, wd_ref[...], preferred_element_type=F32)

        @pl.when(j == 0)
        def _():
            acc_ref[...] = part

        @pl.when(j > 0)
        def _():
            acc_ref[...] += part

        @pl.when(j == nj - 1)
        def _():
            o_ref[...] = acc_ref[...].astype(o_ref.dtype)

    @pl.when(jnp.logical_not(live) & (j == nj - 1))
    def _():
        o_ref[...] = jnp.zeros(o_ref.shape, o_ref.dtype)


def _ffn_grouped(h, tile_expert, n_tiles, wg, wu, wd, *, tm, tf):
    R, D = h.shape
    F = wg.shape[-1]
    nt_max = R // tm
    nj = F // tf

    def live_tile(i, nt):
        return jnp.minimum(i, jnp.maximum(nt[0] - 1, 0))

    grid_spec = pltpu.PrefetchScalarGridSpec(
        num_scalar_prefetch=2,
        grid=(nt_max, nj),
        in_specs=[
            pl.BlockSpec((tm, D), lambda i, j, te, nt: (live_tile(i, nt), 0)),
            pl.BlockSpec((None, D, tf), lambda i, j, te, nt: (te[live_tile(i, nt)], 0, j)),
            pl.BlockSpec((None, D, tf), lambda i, j, te, nt: (te[live_tile(i, nt)], 0, j)),
            pl.BlockSpec((None, tf, D), lambda i, j, te, nt: (te[live_tile(i, nt)], j, 0)),
        ],
        out_specs=pl.BlockSpec((tm, D), lambda i, j, te, nt: (i, 0)),
        scratch_shapes=[pltpu.VMEM((tm, D), F32)],
    )
    return pl.pallas_call(
        _ffn_kernel,
        grid_spec=grid_spec,
        out_shape=jax.ShapeDtypeStruct((R, D), BF16),
        compiler_params=_cparams(("arbitrary", "arbitrary")),
    )(tile_expert, n_tiles, h, wg, wu, wd)


def _combine_kernel(x_ref, g2_ref, *rest, n_terms, weighted, final):
    ys = rest[:n_terms]
    ws = rest[n_terms:2 * n_terms] if weighted else ()
    rest = rest[n_terms + len(ws):]
    if final:
        fg_ref, o_ref = rest
    else:
        (o_ref,) = rest
    f = None
    for k in range(n_terms):
        term = ys[k][...].astype(F32)
        if weighted:
            term = ws[k][...] * term
        f = term if f is None else f + term
    x = x_ref[...] + g2_ref[...] * f
    if final:
        x = (x * lax.rsqrt(jnp.mean(x * x, axis=-1, keepdims=True) + EPS)) * fg_ref[...]
    o_ref[...] = x


def _combine(x, mod, l, ys, ws, final_g, *, tm):
    N, D = x.shape
    r = lambda w: pl.BlockSpec((tm, w), lambda i: (i, 0))
    ins = [r(D), mod.spec(l, 5, tm)] + [r(D)] * len(ys) + [r(1)] * len(ws)
    args = [x, mod.arr, *ys, *ws]
    if final_g is not None:
        ins.append(pl.BlockSpec(final_g.shape, lambda i: (0, 0)))
        args.append(final_g)
    return pl.pallas_call(
        functools.partial(_combine_kernel, n_terms=len(ys), weighted=bool(ws), final=final_g is not None),
        grid=(N // tm,),
        in_specs=ins,
        out_specs=r(D),
        out_shape=jax.ShapeDtypeStruct((N, D), F32),
        compiler_params=_cparams(("arbitrary",)),
    )(*args)


def _route(logits, n_experts, tm):
    N = logits.shape[0]
    top_v, top_i = lax.top_k(logits, TOP_K)
    w = jax.nn.softmax(top_v, axis=-1)
    flat_e = top_i.reshape(-1).astype(jnp.int32)
    n_slots = flat_e.shape[0]
    onehot = (flat_e[:, None] == jnp.arange(n_experts, dtype=jnp.int32)[None, :]).astype(jnp.int32)
    csum = jnp.cumsum(onehot, axis=0)
    counts = csum[-1]
    rank = jnp.sum(onehot * csum, axis=1) - 1
    padded = ((counts + tm - 1) // tm) * tm
    pad_end = jnp.cumsum(padded)
    pad_start = pad_end - padded
    start = jnp.cumsum(counts) - counts
    pos = (jnp.sum(onehot * pad_start[None, :], axis=1) + rank).reshape(N, TOP_K)
    R = n_slots + n_experts * tm
    tile_ids = jnp.arange(R // tm, dtype=jnp.int32)
    tile_expert = jnp.minimum(jnp.searchsorted(pad_end // tm, tile_ids, side="right"), n_experts - 1).astype(jnp.int32)
    n_tiles = (pad_end[-1] // tm).reshape(1).astype(jnp.int32)
    order = jnp.argsort(flat_e, stable=True).astype(jnp.int32)
    row_e = jnp.repeat(tile_expert, tm)
    row_rank = jnp.arange(R, dtype=jnp.int32) - pad_start[row_e]
    src_slot = order[jnp.clip(start[row_e] + row_rank, 0, n_slots - 1)]
    src_token = jnp.where(row_rank < counts[row_e], src_slot // TOP_K, 0)
    return w, src_token, pos, tile_expert, n_tiles


def _prep_weights(p):
    HW = p["w_branch_a"].shape[1]
    w_in = p["w_in"]
    n_f = p["b_fox_f"].shape[-1]
    q_end = 6 * HW
    wf = jnp.pad(w_in[:, :, q_end:q_end + n_f], ((0, 0), (0, 0), (0, LANES - n_f)))
    router = p["moe_router"]
    return dict(
        wqkv=w_in[:, :, :q_end].astype(BF16),
        wf=wf.astype(BF16),
        wgate=w_in[:, :, q_end + n_f:].astype(BF16),
        wa=p["w_branch_a"].astype(BF16), wb=p["w_branch_b"].astype(BF16), wo=p["w_out"].astype(BF16),
        ffn_g=p["ffn_w_gate"].astype(BF16), ffn_u=p["ffn_w_up"].astype(BF16), ffn_d=p["ffn_w_down"].astype(BF16),
        moe_g=p["moe_w_gate"].astype(BF16), moe_u=p["moe_w_up"].astype(BF16), moe_d=p["moe_w_down"].astype(BF16),
        router=jnp.pad(router, ((0, 0), (0, 0), (0, LANES - router.shape[-1]))),
    )


def _trunk(x, mod, p, w, attn_fn, *, tm, tm_moe, tf, emit_bf16, q_scale_a, q_scale_b):
    depth = p["w_in"].shape[0]
    n_experts = p["moe_router"].shape[-1]
    rows = [[], [], [], [], []]
    for l in range(depth):
        pr = _inproj(x, mod, l, p["norm_attn_g"][l][None], w["wqkv"][l], w["wf"][l], w["wgate"][l],
                     p["b_fox_f"][l][None], emit_bf16=emit_bf16, scale_a=q_scale_a, scale_b=q_scale_b, tm=tm)
        qa, ka, va, qb, kb, vb, logf, sga, sgb = pr[:9]
        ya, yb = attn_fn(l, pr)
        x = _post_attn(ya, yb, sga, sgb, x, mod, l, w["wa"][l], w["wb"][l], w["wo"][l], tm=tm)
        final_g = p["final_norm_g"][None] if l == depth - 1 else None
        i = l // 2
        if l % 2 == 0:
            h, _ = _ffn_norm(x, mod, l, p["norm_ffn_g"][l][None], None, tm=tm)
            N = h.shape[0]
            tmd = _pick_tile(N, tm_moe)
            y = _ffn_grouped(h, jnp.zeros((N // tmd,), jnp.int32), jnp.full((1,), N // tmd, jnp.int32),
                             w["ffn_g"][i:i + 1], w["ffn_u"][i:i + 1], w["ffn_d"][i:i + 1], tm=tmd, tf=tf)
            x = _combine(x, mod, l, [y], [], final_g, tm=tm)
        else:
            h, logits = _ffn_norm(x, mod, l, p["norm_ffn_g"][l][None], w["router"][i], tm=tm)
            gate_w, src_token, pos, tile_expert, n_tiles = _route(logits[:, :n_experts], n_experts, tm_moe)
            y = _ffn_grouped(jnp.take(h, src_token, axis=0, mode="clip"), tile_expert, n_tiles,
                             w["moe_g"][i], w["moe_u"][i], w["moe_d"][i], tm=tm_moe, tf=tf)
            ys = [jnp.take(y, pos[:, k], axis=0, mode="clip") for k in range(TOP_K)]
            ws = [gate_w[:, k:k + 1] for k in range(TOP_K)]
            x = _combine(x, mod, l, ys, ws, final_g, tm=tm)
        for r, n in zip(rows, (ka, va, kb, vb, logf)):
            r.append(n)
    return x, [jnp.stack(r) for r in rows]


def kernel(x_prompt, x_sample, c_prompt, c_sample, cache_diff_k, cache_diff_v, cache_fox_k, cache_fox_v, cache_fox_logf, page_table, w_ada, b_ada, norm_attn_g, norm_ffn_g, w_in, b_fox_f, lambda_q1, lambda_k1, lambda_q2, lambda_k2, subln_g, rel_bias, w_branch_a, w_branch_b, w_out, ffn_w_gate, ffn_w_up, ffn_w_down, moe_router, moe_w_gate, moe_w_up, moe_w_down, final_norm_g):
    p = dict(w_ada=w_ada, b_ada=b_ada, norm_attn_g=norm_attn_g, norm_ffn_g=norm_ffn_g, w_in=w_in,
             b_fox_f=b_fox_f, subln_g=subln_g, rel_bias=rel_bias, w_branch_a=w_branch_a,
             w_branch_b=w_branch_b, w_out=w_out, ffn_w_gate=ffn_w_gate, ffn_w_up=ffn_w_up,
             ffn_w_down=ffn_w_down, moe_router=moe_router, moe_w_gate=moe_w_gate, moe_w_up=moe_w_up,
             moe_w_down=moe_w_down, final_norm_g=final_norm_g)
    B, T, D = x_prompt.shape
    Bs, S, _ = x_sample.shape
    depth, n_phys, page, HA, DK2 = cache_diff_k.shape
    HB, DH = cache_fox_k.shape[3:]
    DK = DK2 // 2
    n_pages = page_table.shape[1]
    past_len = n_pages * page
    W = HA * DK2
    assert DK2 == LANES and 2 * DH == LANES and HB * DH == W and cache_diff_v.shape[-1] == LANES
    G = W // LANES
    d_ff = ffn_w_gate.shape[-1]
    tf = d_ff // 2 if (d_ff // 2) % LANES == 0 else d_ff

    w = _prep_weights(p)
    bp = B + (-B) % SUBLANES
    mod_p, mod_s = _ada_mod(jnp.pad(c_prompt, ((0, bp - B), (0, 0))), c_sample, w_ada, b_ada)
    mod_p = _Mod(mod_p.reshape(-1, 1, D), T, nb=bp)
    mod_s = _Mod(jnp.repeat(mod_s, S, axis=1), 1)

    lam_init = np.array([0.8 - 0.6 * math.exp(-0.3 * l) for l in range(depth)], np.float32)
    lam = (jnp.exp(jnp.sum(lambda_q1 * lambda_k1, axis=-1)) - jnp.exp(jnp.sum(lambda_q2 * lambda_k2, axis=-1))
           + lam_init)

    t = _pick_tile(T, 256)
    key_ = np.arange(t)[:, None]
    qry_ = (np.arange(2 * t) % t)[None, :]
    far_b = np.unique(_t5_bucket_np(np.arange(t + 1, max(T, t + 2))))
    assert far_b.size == 1, "far key blocks must share one bias bucket"
    bkt = np.concatenate([_t5_bucket_np(qry_ - key_), _t5_bucket_np(t + qry_ - key_)], axis=0)
    c_far = rel_bias[int(far_b[0])]
    b01 = (_bias_lookup(rel_bias, bkt) - c_far[:, None, None]) * LOG2E
    b0, b1 = b01[:, :t], b01[:, t:]
    kpos = np.arange(past_len + page)[None, :]
    qpos = past_len + np.tile(np.arange(S), 2)[:, None]
    bias_s = _bias_lookup(rel_bias, _t5_bucket_np(qpos - kpos))

    def scal(l):
        return jnp.concatenate([c_far, lam[l][None], jnp.full((1,), 1.0 - float(lam_init[l]), F32)]).astype(F32)

    def attn_prompt(l, pr):
        qa, qb, logf = pr[0], pr[3], pr[6]
        kab, vab, kbb, vbb = pr[9:13]
        r3 = lambda a: a.reshape(B, T, W)
        rt = lambda a: a.reshape(B, T // t, W, t)
        ya = _flash_prompt("diff", rt(qa), r3(kab), rt(vab), scal(l), (b0, b1, subln_g[l][:, None]), t=t)
        cum = jnp.cumsum(logf.reshape(B, T, HB), axis=1)
        kx, qx = _decay_columns(cum.reshape(B, T, G, 2) * LOG2E)
        yb = _flash_prompt("fox", rt(qb), r3(kbb), rt(vbb), jnp.zeros((1,), F32), (kx, qx), t=t)
        return ya.reshape(B * T, W), yb.reshape(B * T, W)

    tm_p = t
    y_p, rows_p = _trunk(x_prompt.reshape(B * T, D), mod_p, p, w, attn_prompt,
                         tm=tm_p, tm_moe=_pick_tile(T, 512), tf=tf, emit_bf16=True,
                         q_scale_a=DK ** -0.5 * LOG2E, q_scale_b=DH ** -0.5 * LOG2E)

    diff_view = lambda c: c.reshape(depth, n_phys, page * HA, DK2)
    fox_view = lambda c: c.transpose(0, 1, 3, 4, 2).reshape(depth, n_phys, HB * DH, page)
    cdk, cdv, cfk, cfv = diff_view(cache_diff_k), diff_view(cache_diff_v), fox_view(cache_fox_k), fox_view(cache_fox_v)

    def attn_sample(l, pr):
        qa, ka, va, qb, kb, vb, logf = pr[:7]
        ya = _attn_sample("diff", qa, ka, va, scal(l), (bias_s, subln_g[l][None]), cdk, cdv, l, page_table)
        past_lf = cache_fox_logf[l][page_table].reshape(Bs, past_len, HB)
        cum = jnp.cumsum(jnp.concatenate([past_lf, logf.reshape(Bs, S, HB)], axis=1), axis=1)
        cq = cum[:, past_len:]
        ck = jnp.pad(cum, ((0, 0), (0, page - S), (0, 0))).transpose(0, 2, 1)
        yb = _attn_sample("fox", qb, kb, vb, jnp.zeros((1,), F32), (cq, ck), cfk, cfv, l, page_table)
        return ya, yb

    Ns = Bs * S
    y_s, rows_s = _trunk(x_sample.reshape(Ns, D), mod_s, p, w, attn_sample,
                         tm=_pick_tile(Ns, 256), tm_moe=_pick_tile(Ns, 256), tf=tf, emit_bf16=False,
                         q_scale_a=DK ** -0.5, q_scale_b=DH ** -0.5)

    def shape_rows(rows, b, s):
        dk_, dv_, fk_, fv_, lf_ = rows
        return (dk_.reshape(depth, b, s, HA, DK2), dv_.reshape(depth, b, s, HA, LANES),
                fk_.reshape(depth, b, s, HB, DH), fv_.reshape(depth, b, s, HB, DH), lf_.reshape(depth, b, s, HB))

    return (y_p.reshape(B, T, D), y_s.reshape(Bs, S, D)) + shape_rows(rows_p, B, T) + shape_rows(rows_s, Bs, S)
```

```python
import functools
import math

import jax
import jax.numpy as jnp
import numpy as np
from jax import lax
from jax.experimental import pallas as pl
from jax.experimental.pallas import tpu as pltpu

F32 = jnp.float32
BF16 = jnp.bfloat16

EPS = 1e-6
NEG_INF = -1e30
NUM_BUCKETS = 32
MAX_DISTANCE = 128
TOP_K = 2
LOG2E = 1.4426950408889634
FLASH_TILE = 256

LANES = 128
SUBLANES = 8
VMEM_LIMIT = 56 * 1024 * 1024


def _cparams(sem):
    return pltpu.CompilerParams(dimension_semantics=sem, vmem_limit_bytes=VMEM_LIMIT)


def _pick_tile(n, pref):
    t = min(n, pref)
    while n % t:
        t //= 2
    return t


def _t5_bucket_np(rel):
    n = np.maximum(rel, 0)
    max_exact = NUM_BUCKETS // 2
    nf = np.maximum(n, 1).astype(np.float32)
    large = max_exact + (np.log(nf / np.float32(max_exact)) / np.float32(math.log(MAX_DISTANCE / max_exact))
                         * np.float32(NUM_BUCKETS - max_exact)).astype(np.int32)
    large = np.minimum(large, NUM_BUCKETS - 1)
    return np.where(n < max_exact, n, large).astype(np.int32)


def _ada_kernel(cp_ref, cs_ref, w_ref, b_ref, op_ref, os_ref):
    w = w_ref[...].astype(BF16)
    b = b_ref[...]
    op_ref[...] = jnp.dot(cp_ref[...].astype(BF16), w, preferred_element_type=F32) + b
    os_ref[...] = jnp.dot(cs_ref[...].astype(BF16), w, preferred_element_type=F32) + b


def _ada_mod(c_p, c_s, w_ada, b_ada):
    L, D, D6 = w_ada.shape
    nj = D6 // D
    bp, bs = c_p.shape[0], c_s.shape[0]
    b3 = b_ada.reshape(L, 1, D6)
    return pl.pallas_call(
        _ada_kernel,
        grid=(L, nj),
        in_specs=[
            pl.BlockSpec((bp, D), lambda l, j: (0, 0)),
            pl.BlockSpec((bs, D), lambda l, j: (0, 0)),
            pl.BlockSpec((None, D, D), lambda l, j: (l, 0, j)),
            pl.BlockSpec((None, 1, D), lambda l, j: (l, 0, j)),
        ],
        out_specs=[
            pl.BlockSpec((None, bp, D), lambda l, j: (l * nj + j, 0, 0)),
            pl.BlockSpec((None, bs, D), lambda l, j: (l * nj + j, 0, 0)),
        ],
        out_shape=[jax.ShapeDtypeStruct((L * nj, bp, D), F32),
                   jax.ShapeDtypeStruct((L * nj, bs, D), F32)],
        compiler_params=_cparams(("arbitrary", "arbitrary")),
    )(c_p, c_s, w_ada, b3)


class _Mod:
    def __init__(self, arr, rows_per_mod_row, nb=1):
        self.arr = arr
        self.rows_per = rows_per_mod_row
        self.nb = nb

    def spec(self, l, j, tm):
        D = self.arr.shape[-1]
        idx = l * 6 + j
        if self.rows_per == 1:
            return pl.BlockSpec((None, tm, D), lambda i, *_: (idx, i, 0))
        assert self.rows_per % tm == 0
        per = self.rows_per // tm
        nb = self.nb
        return pl.BlockSpec((None, 1, D), lambda i, *_: (idx * nb + i // per, 0, 0))


def _rmsnorm_mod(x, g, sc, sh):
    y = x * lax.rsqrt(jnp.mean(x * x, axis=-1, keepdims=True) + EPS)
    return (y * g) * (1.0 + sc) + sh


def _inproj_kernel(x_ref, sc_ref, sh_ref, g_ref, wqkv_ref, wf_ref, wg_ref, bf_ref, *outs,
                   hw, n_f, emit_bf16, scale_a, scale_b):
    if emit_bf16:
        (qa_ref, ka_ref, va_ref, qb_ref, kb_ref, vb_ref, lf_ref, sga_ref, sgb_ref,
         kab_ref, vab_ref, kbb_ref, vbb_ref) = outs
    else:
        qa_ref, ka_ref, va_ref, qb_ref, kb_ref, vb_ref, lf_ref, sga_ref, sgb_ref = outs
        kab_ref = vab_ref = kbb_ref = vbb_ref = None
    h = _rmsnorm_mod(x_ref[...], g_ref[...], sc_ref[...], sh_ref[...]).astype(BF16)

    def proj(c):
        return jnp.dot(h, wqkv_ref[:, c * hw:(c + 1) * hw], preferred_element_type=F32)

    for c, o, scale in ((0, qa_ref, scale_a), (3, qb_ref, scale_b)):
        y = proj(c) * scale
        o[...] = (y.T if emit_bf16 else y).astype(o.dtype)
    for c, o32, o16, transposed in ((1, ka_ref, kab_ref, False), (2, va_ref, vab_ref, True),
                                    (4, kb_ref, kbb_ref, False), (5, vb_ref, vbb_ref, True)):
        y = proj(c)
        o32[...] = y
        if o16 is not None:
            o16[...] = (y.T if transposed else y).astype(BF16)
    z = jnp.dot(h, wf_ref[...], preferred_element_type=F32)[:, :n_f] + bf_ref[...]
    lf_ref[...] = jnp.minimum(z, 0.0) - jnp.log(1.0 + jnp.exp(-jnp.abs(z)))
    D = sga_ref.shape[-1]
    ga = jnp.dot(h, wg_ref[:, :D], preferred_element_type=F32)
    sga_ref[...] = jax.nn.sigmoid(ga).astype(sga_ref.dtype)
    gb = jnp.dot(h, wg_ref[:, D:], preferred_element_type=F32)
    sgb_ref[...] = jax.nn.sigmoid(gb).astype(sgb_ref.dtype)


def _inproj(x, mod, l, g_attn, wqkv, wf, wg, b_f, *, emit_bf16, scale_a, scale_b, tm, chunk):
    N, D = x.shape
    hw = wqkv.shape[1] // 6
    n_f = b_f.shape[-1]
    row = lambda w, dt: (pl.BlockSpec((tm, w), lambda i: (i, 0)), jax.ShapeDtypeStruct((N, w), dt))
    per = chunk // tm
    vt = (pl.BlockSpec((None, hw, tm), lambda i: (i // per, 0, i % per)),
          jax.ShapeDtypeStruct((N // chunk, hw, chunk), BF16))
    qo = vt if emit_bf16 else row(hw, F32)
    outs = [qo, row(hw, F32), row(hw, F32), qo, row(hw, F32), row(hw, F32),
            row(n_f, F32), row(D, BF16), row(D, BF16)]
    if emit_bf16:
        outs += [row(hw, BF16), vt, row(hw, BF16), vt]
    full = lambda a: pl.BlockSpec(a.shape, lambda i: (0,) * a.ndim)
    return pl.pallas_call(
        functools.partial(_inproj_kernel, hw=hw, n_f=n_f, emit_bf16=emit_bf16,
                          scale_a=scale_a, scale_b=scale_b),
        grid=(N // tm,),
        in_specs=[pl.BlockSpec((tm, D), lambda i: (i, 0)), mod.spec(l, 1, tm), mod.spec(l, 0, tm),
                  full(g_attn), full(wqkv), full(wf), full(wg), full(b_f)],
        out_specs=[o[0] for o in outs],
        out_shape=[o[1] for o in outs],
        compiler_params=_cparams(("arbitrary",)),
    )(x, mod.arr, mod.arr, g_attn, wqkv, wf, wg, b_f)


def _two_map_queries(q):
    lane = lax.broadcasted_iota(jnp.int32, q.shape, 1)
    half = q.shape[-1] // 2
    zero = jnp.zeros_like(q)
    return jnp.where(lane < half, q, zero), jnp.where(lane >= half, q, zero)


def _finish_diff(o1, o2, lam, out_scale, g):
    y = o1 - lam * o2
    y = y * lax.rsqrt(jnp.mean(y * y, axis=-1, keepdims=True) + EPS)
    return (y * g) * out_scale


def _finish_fox(o1, o2):
    lane = lax.broadcasted_iota(jnp.int32, o1.shape, 1)
    return jnp.where(lane < o1.shape[-1] // 2, o1, o2)


def _split3_bf16(c):
    def trunc(x):
        bits = lax.bitcast_convert_type(x, jnp.uint32) & jnp.uint32(0xFFFF0000)
        return lax.bitcast_convert_type(bits, F32)
    hi = trunc(c)
    r = c - hi
    mid = trunc(r)
    return hi.astype(BF16), mid.astype(BF16), (r - mid).astype(BF16)


def _decay_columns(cum):
    B, T, G, _ = cum.shape
    parts = jnp.stack(_split3_bf16(cum), axis=-1)
    one = jnp.ones((B, T, G, 3), BF16)
    zero = jnp.zeros((B, T, G, 6), BF16)
    kx = jnp.concatenate([-parts[..., 0, :], one, -parts[..., 1, :], one], axis=-1)
    q0 = jnp.concatenate([one, parts[..., 0, :], zero], axis=-1)
    q1 = jnp.concatenate([zero, one, parts[..., 1, :]], axis=-1)
    padl = lambda a: jnp.pad(a, [(0, 0)] * (a.ndim - 1) + [(0, LANES - a.shape[-1])])
    kx = padl(kx).transpose(0, 2, 1, 3)
    qx = jnp.stack([padl(q0), padl(q1)], axis=0).transpose(1, 3, 0, 4, 2)
    return kx, qx


def _flash_prompt_kernel(scal_ref, q_ref, k_ref, vt_ref, *rest, mode, t):
    if mode == "diff":
        b0_ref, b1_ref, g_ref = rest[:3]
    else:
        kx_ref, qx_ref = rest[:2]
    o_ref, m_sc, l_sc, acc_sc, sa_sc, sb_sc, pb_sc, ab_sc = rest[-8:]
    qi = pl.program_id(2)
    qt = q_ref[...]
    feat = lax.broadcasted_iota(jnp.int32, qt.shape, 0)
    zero = jnp.zeros_like(qt)
    qq = jnp.concatenate([jnp.where(feat < LANES // 2, qt, zero),
                          jnp.where(feat >= LANES // 2, qt, zero)], axis=1)
    if mode == "fox":
        qq = jnp.concatenate([qq, jnp.concatenate([qx_ref[0], qx_ref[1]], axis=1)], axis=0)
    m_sc[...] = jnp.full(m_sc.shape, NEG_INF, F32)
    l_sc[...] = jnp.zeros(l_sc.shape, F32)
    acc_sc[...] = jnp.zeros(acc_sc.shape, F32)

    def scores(kj, kind):
        ks = pl.multiple_of(kj * t, t)
        k = k_ref[pl.ds(ks, t), :]
        if mode == "fox":
            k = jnp.concatenate([k, kx_ref[pl.ds(ks, t), :]], axis=1)
        s = jnp.dot(k, qq, preferred_element_type=F32)
        if kind == "diag":
            key = lax.broadcasted_iota(jnp.int32, (t, 2 * t), 0)
            qry = lax.broadcasted_iota(jnp.int32, (t, 2 * t), 1)
            qry = jnp.where(qry >= t, qry - t, qry)
            visible = key <= qry
        if mode == "diff":
            if kind == "near":
                s = s + b1_ref[...]
            elif kind == "diag":
                s = jnp.where(visible, s + b0_ref[...], NEG_INF)
        elif kind == "diag":
            s = jnp.where(visible, s, NEG_INF)
        return s

    def softmax_update(s):
        m_prev = m_sc[...]
        m_new = jnp.maximum(m_prev, jnp.max(s, axis=0, keepdims=True))
        alpha = jnp.exp2(m_prev - m_new)
        p = jnp.exp2(s - m_new)
        l_sc[...] = alpha * l_sc[...] + jnp.sum(p, axis=0, keepdims=True)
        m_sc[...] = m_new
        return p.astype(BF16), alpha

    def pv_update(kj, p, alpha):
        acc_sc[...] = alpha * acc_sc[...] + jnp.dot(vt_ref[kj], p, preferred_element_type=F32)

    def step(kj, kind):
        p, alpha = softmax_update(scores(kj, kind))
        pv_update(kj, p, alpha)

    n_far = jnp.maximum(qi - 1 if mode == "diff" else qi, 0)
    n_pairs = lax.shift_right_logical(n_far, 1)
    pb_sc[...] = jnp.zeros(pb_sc.shape, BF16)
    ab_sc[...] = jnp.ones(ab_sc.shape, F32)

    @pl.when(n_pairs > 0)
    def _():
        sa_sc[...] = scores(0, "far")

    def pair_body(i, c):
        a = 2 * i
        sb_sc[...] = scores(a + 1, "far")
        pv_update(jnp.maximum(a - 1, 0), pb_sc[...], ab_sc[...])
        p_a, alpha_a = softmax_update(sa_sc[...])
        sa_sc[...] = scores(jnp.minimum(a + 2, 2 * (n_pairs - 1)), "far")
        p_b, alpha_b = softmax_update(sb_sc[...])
        pv_update(a, p_a, alpha_a)
        pb_sc[...] = p_b
        ab_sc[...] = alpha_b
        return c

    lax.fori_loop(0, n_pairs, pair_body, 0)
    pv_update(jnp.maximum(2 * n_pairs - 1, 0), pb_sc[...], ab_sc[...])

    def two_steps(kj_a, kind_a, kj_b, kind_b):
        s_a = scores(kj_a, kind_a)
        s_b = scores(kj_b, kind_b)
        p_a, alpha_a = softmax_update(s_a)
        p_b, alpha_b = softmax_update(s_b)
        pv_update(kj_a, p_a, alpha_a)
        pv_update(kj_b, p_b, alpha_b)

    odd_far = n_far - 2 * n_pairs == 1
    if mode == "diff":
        @pl.when(odd_far)
        def _():
            step(n_far - 1, "far")

        @pl.when(qi >= 1)
        def _():
            two_steps(qi - 1, "near", qi, "diag")

        @pl.when(qi == 0)
        def _():
            step(qi, "diag")
    else:
        @pl.when(odd_far)
        def _():
            two_steps(n_far - 1, "far", qi, "diag")

        @pl.when(jnp.logical_not(odd_far))
        def _():
            step(qi, "diag")

    o = acc_sc[...] / l_sc[...]
    o1, o2 = o[:, :t], o[:, t:]
    if mode == "diff":
        nh = scal_ref.shape[0] - 2
        y = o1 - scal_ref[nh] * o2
        y = y * lax.rsqrt(jnp.mean(y * y, axis=0, keepdims=True) + EPS)
        y = (y * g_ref[...]) * scal_ref[nh + 1]
    else:
        feat = lax.broadcasted_iota(jnp.int32, o1.shape, 0)
        y = jnp.where(feat < LANES // 2, o1, o2)
    o_ref[...] = y.T.astype(o_ref.dtype)


def _flash_prompt(mode, qt, k, vt, scal, extra, *, t):
    B, T, W = k.shape
    G = W // LANES
    nq = T // t
    qspec = pl.BlockSpec((None, None, LANES, t), lambda b, g, i: (b, i, g, 0))
    ospec = pl.BlockSpec((None, t, LANES), lambda b, g, i: (b, i, g))
    kspec = pl.BlockSpec((None, T, LANES), lambda b, g, i: (b, 0, g))
    vspec = pl.BlockSpec((None, nq, LANES, t), lambda b, g, i: (b, 0, g, 0))
    if mode == "diff":
        especs = [pl.BlockSpec((None, t, 2 * t), lambda b, g, i: (g, 0, 0)),
                  pl.BlockSpec((None, t, 2 * t), lambda b, g, i: (g, 0, 0)),
                  pl.BlockSpec((LANES, 1), lambda b, g, i: (0, 0))]
    else:
        especs = [pl.BlockSpec((None, None, T, LANES), lambda b, g, i: (b, g, 0, 0)),
                  pl.BlockSpec((None, None, 2, LANES, t), lambda b, g, i: (b, g, 0, 0, i))]
    return pl.pallas_call(
        functools.partial(_flash_prompt_kernel, mode=mode, t=t),
        grid=(B, G, nq),
        in_specs=[pl.BlockSpec(memory_space=pltpu.SMEM), qspec, kspec, vspec] + especs,
        out_specs=ospec,
        out_shape=jax.ShapeDtypeStruct((B, T, W), BF16),
        scratch_shapes=[pltpu.VMEM((1, 2 * t), F32), pltpu.VMEM((1, 2 * t), F32),
                        pltpu.VMEM((LANES, 2 * t), F32),
                        pltpu.VMEM((t, 2 * t), F32), pltpu.VMEM((t, 2 * t), F32),
                        pltpu.VMEM((t, 2 * t), BF16), pltpu.VMEM((1, 2 * t), F32)],
        compiler_params=_cparams(("arbitrary", "arbitrary", "arbitrary")),
    )(scal, qt, k, vt, *extra)


def _attn_sample_kernel(pt_ref, scal_ref, q_ref, kn_ref, vn_ref, *rest, mode, n_pages, page, n_groups):
    del pt_ref
    if mode == "diff":
        bias_ref, g_ref = rest[:2]
        rest = rest[2:]
    else:
        cq_ref, ck_ref = rest[:2]
        rest = rest[2:]
    k_refs = rest[:n_pages]
    v_refs = rest[n_pages:2 * n_pages]
    o_ref = rest[2 * n_pages]
    S = q_ref.shape[0]
    nt_dims = (((1,), (1,)), ((), ()))
    rows = 2 * S
    nblk = n_pages + 1
    r_i = lax.broadcasted_iota(jnp.int32, (rows, page), 0)
    c_i = lax.broadcasted_iota(jnp.int32, (rows, page), 1)
    tok = jnp.where(r_i >= S, r_i - S, r_i)
    new_ok = (c_i < S) & (c_i <= tok)
    top = lax.broadcasted_iota(jnp.int32, (rows, 1), 0) < S
    pad = jnp.zeros((page - S, LANES), F32)
    for g in range(n_groups):
        ls = slice(g * LANES, (g + 1) * LANES)
        q1, q2 = _two_map_queries(q_ref[:, ls])
        q2m = jnp.concatenate([q1, q2], axis=0).astype(BF16)
        kn = jnp.concatenate([kn_ref[:, ls], pad], axis=0).astype(BF16)
        vn = jnp.concatenate([vn_ref[:, ls], pad], axis=0).astype(BF16)
        s_blocks = []
        for p in range(nblk):
            if p == n_pages:
                s = lax.dot_general(q2m, kn, nt_dims, preferred_element_type=F32)
            elif mode == "diff":
                kp = k_refs[p][pl.ds(g, page, stride=n_groups), :].astype(BF16)
                s = lax.dot_general(q2m, kp, nt_dims, preferred_element_type=F32)
            else:
                kp = k_refs[p][g * LANES:(g + 1) * LANES, :].astype(BF16)
                s = jnp.dot(q2m, kp, preferred_element_type=F32)
            s_blocks.append(s)
        if mode == "diff":
            add = [bias_ref[g, :, p * page:(p + 1) * page] for p in range(nblk)]
        else:
            cq = jnp.where(top, jnp.concatenate([cq_ref[:, 2 * g:2 * g + 1]] * 2, axis=0),
                           jnp.concatenate([cq_ref[:, 2 * g + 1:2 * g + 2]] * 2, axis=0))
            add = []
            for p in range(nblk):
                cs = slice(p * page, (p + 1) * page)
                ck = jnp.where(top, ck_ref[2 * g:2 * g + 1, cs], ck_ref[2 * g + 1:2 * g + 2, cs])
                add.append(cq - ck)
        s_blocks = [s + a for s, a in zip(s_blocks, add)]
        s_blocks[n_pages] = jnp.where(new_ok, s_blocks[n_pages], NEG_INF)
        m = s_blocks[0].max(axis=1, keepdims=True)
        for s in s_blocks[1:]:
            m = jnp.maximum(m, s.max(axis=1, keepdims=True))
        l = jnp.zeros((rows, 1), F32)
        acc = jnp.zeros((rows, LANES), F32)
        for p in range(nblk):
            pr = jnp.exp(s_blocks[p] - m)
            l = l + pr.sum(axis=1, keepdims=True)
            prb = pr.astype(BF16)
            if p == n_pages:
                acc = acc + jnp.dot(prb, vn, preferred_element_type=F32)
            elif mode == "diff":
                vp = v_refs[p][pl.ds(g, page, stride=n_groups), :].astype(BF16)
                acc = acc + jnp.dot(prb, vp, preferred_element_type=F32)
            else:
                vp = v_refs[p][g * LANES:(g + 1) * LANES, :].astype(BF16)
                acc = acc + lax.dot_general(prb, vp, nt_dims, preferred_element_type=F32)
        o = acc / l
        o1, o2 = o[:S], o[S:]
        if mode == "diff":
            nh = scal_ref.shape[0] - 2
            y = _finish_diff(o1, o2, scal_ref[nh], scal_ref[nh + 1], g_ref[...])
        else:
            y = _finish_fox(o1, o2)
        o_ref[:, ls] = y.astype(o_ref.dtype)


def _attn_sample(mode, q, kn, vn, scal, extra, cache_k, cache_v, l, page_table):
    Bs, n_pages = page_table.shape
    W = q.shape[1]
    S = q.shape[0] // Bs
    n_groups = W // LANES
    page = cache_k.shape[2] // n_groups
    assert page == LANES
    rowspec = pl.BlockSpec((S, W), lambda b, pt: (b, 0))
    if mode == "diff":
        bias, sg = extra
        especs = [pl.BlockSpec(bias.shape, lambda b, pt: (0, 0, 0)),
                  pl.BlockSpec((1, LANES), lambda b, pt: (0, 0))]
    else:
        cq, ck = extra
        especs = [pl.BlockSpec((None,) + cq.shape[1:], lambda b, pt: (b, 0, 0)),
                  pl.BlockSpec((None,) + ck.shape[1:], lambda b, pt: (b, 0, 0))]
    pspecs = [pl.BlockSpec((None, None, page * n_groups, LANES), lambda b, pt, p=p: (l, pt[b, p], 0, 0))
              for p in range(n_pages)]
    grid_spec = pltpu.PrefetchScalarGridSpec(
        num_scalar_prefetch=1,
        grid=(Bs,),
        in_specs=[pl.BlockSpec(memory_space=pltpu.SMEM), rowspec, rowspec, rowspec] + especs + pspecs + pspecs,
        out_specs=rowspec,
    )
    return pl.pallas_call(
        functools.partial(_attn_sample_kernel, mode=mode, n_pages=n_pages, page=page, n_groups=n_groups),
        grid_spec=grid_spec,
        out_shape=jax.ShapeDtypeStruct(q.shape, F32),
        compiler_params=_cparams(("arbitrary",)),
    )(page_table, scal, q, kn, vn, *extra, *([cache_k] * n_pages), *([cache_v] * n_pages))


def _bias_kernel(tab_ref, bkt_ref, o_ref, *, n_buckets):
    h = pl.program_id(0)
    bkt = bkt_ref[...]
    acc = jnp.zeros(bkt.shape, F32)
    for b in range(n_buckets):
        acc = jnp.where(bkt == b, tab_ref[b, h], acc)
    o_ref[...] = acc


def _bias_lookup(rel_bias, buckets):
    nb, H = rel_bias.shape
    R, C = buckets.shape
    return pl.pallas_call(
        functools.partial(_bias_kernel, n_buckets=nb),
        grid=(H,),
        in_specs=[pl.BlockSpec(memory_space=pltpu.SMEM), pl.BlockSpec((R, C), lambda h: (0, 0))],
        out_specs=pl.BlockSpec((None, R, C), lambda h: (h, 0, 0)),
        out_shape=jax.ShapeDtypeStruct((H, R, C), F32),
        compiler_params=_cparams(("arbitrary",)),
    )(rel_bias, jnp.asarray(buckets))


def _post_attn_kernel(ya_ref, yb_ref, sga_ref, sgb_ref, x_ref, g1_ref, wa_ref, wb_ref, wo_ref, o_ref):
    a = jnp.dot(ya_ref[...].astype(BF16), wa_ref[...], preferred_element_type=F32)
    b = jnp.dot(yb_ref[...].astype(BF16), wb_ref[...], preferred_element_type=F32)
    merged = sga_ref[...].astype(F32) * a + sgb_ref[...].astype(F32) * b
    out = jnp.dot(merged.astype(BF16), wo_ref[...], preferred_element_type=F32)
    o_ref[...] = x_ref[...] + g1_ref[...] * out


def _post_attn(ya, yb, sga, sgb, x, mod, l, wa, wb, wo, *, tm):
    N, D = x.shape
    hw = ya.shape[1]
    full = lambda a: pl.BlockSpec(a.shape, lambda i: (0,) * a.ndim)
    r = lambda w: pl.BlockSpec((tm, w), lambda i: (i, 0))
    return pl.pallas_call(
        _post_attn_kernel,
        grid=(N // tm,),
        in_specs=[r(hw), r(hw), r(D), r(D), r(D), mod.spec(l, 2, tm), full(wa), full(wb), full(wo)],
        out_specs=r(D),
        out_shape=jax.ShapeDtypeStruct((N, D), F32),
        compiler_params=_cparams(("arbitrary",)),
    )(ya, yb, sga, sgb, x, mod.arr, wa, wb, wo)


def _ffn_norm_kernel(x_ref, sc_ref, sh_ref, g_ref, *rest, with_router):
    h = _rmsnorm_mod(x_ref[...], g_ref[...], sc_ref[...], sh_ref[...])
    hb = h.astype(BF16)
    if with_router:
        r_ref, h_ref, lg_ref = rest
        h_lo = (h - hb.astype(F32)).astype(BF16)
        r = r_ref[...]
        r_hi = r.astype(BF16)
        r_lo = (r - r_hi.astype(F32)).astype(BF16)
        lg_ref[...] = (jnp.dot(hb, r_hi, preferred_element_type=F32)
                       + jnp.dot(h_lo, r_hi, preferred_element_type=F32)
                       + jnp.dot(hb, r_lo, preferred_element_type=F32))
    else:
        (h_ref,) = rest
    h_ref[...] = h.astype(h_ref.dtype)


def _ffn_norm(x, mod, l, g_ffn, router_pad, *, tm):
    N, D = x.shape
    full = lambda a: pl.BlockSpec(a.shape, lambda i: (0,) * a.ndim)
    r = lambda w: pl.BlockSpec((tm, w), lambda i: (i, 0))
    ins = [r(D), mod.spec(l, 4, tm), mod.spec(l, 3, tm), full(g_ffn)]
    args = [x, mod.arr, mod.arr, g_ffn]
    outs = [(r(D), jax.ShapeDtypeStruct((N, D), BF16 if router_pad is None else F32))]
    if router_pad is not None:
        ins.append(full(router_pad))
        args.append(router_pad)
        outs.append((r(LANES), jax.ShapeDtypeStruct((N, LANES), F32)))
    res = pl.pallas_call(
        functools.partial(_ffn_norm_kernel, with_router=router_pad is not None),
        grid=(N // tm,),
        in_specs=ins,
        out_specs=[o[0] for o in outs],
        out_shape=[o[1] for o in outs],
        compiler_params=_cparams(("arbitrary",)),
    )(*args)
    return res if router_pad is not None else (res[0], None)


def _ffn_kernel(te_ref, nt_ref, *rest, gather):
    del te_ref
    i = pl.program_id(0)
    j = pl.program_id(1)
    nj = pl.num_programs(1)
    live = i < nt_ref[0]
    if gather:
        src_ref, h_hbm, wg_ref, wu_ref, wd_ref, o_ref, acc_ref, xbuf, sem = rest
        tm = xbuf.shape[1]
        slot = lax.rem(i, 2)

        def row_copy(r, s):
            return pltpu.make_async_copy(h_hbm.at[pl.ds(src_ref[0, r], 1)], xbuf.at[s, pl.ds(r, 1)], sem.at[s])

        def start_rows(s):
            def body(r, c):
                row_copy(r, s).start()
                return c
            lax.fori_loop(0, tm, body, 0)

        def wait_rows(s):
            def body(r, c):
                row_copy(r, s).wait()
                return c
            lax.fori_loop(0, tm, body, 0)

        @pl.when(live & (j == 0))
        def _():
            @pl.when(i == 0)
            def _():
                start_rows(slot)
            wait_rows(slot)

        @pl.when((j == nj - 1) & (i + 1 < nt_ref[0]))
        def _():
            start_rows(1 - slot)
    else:
        h_ref, wg_ref, wu_ref, wd_ref, o_ref, acc_ref = rest

    @pl.when(live)
    def _():
        h = xbuf[slot].astype(BF16) if gather else h_ref[...]
        gt = jnp.dot(h, wg_ref[...], preferred_element_type=F32)
        up = jnp.dot(h, wu_ref[...], preferred_element_type=F32)
        a = (jax.nn.silu(gt) * up).astype(BF16)
        part = jnp.dot(a, wd_ref[...], preferred_element_type=F32)

        @pl.when(j == 0)
        def _():
            acc_ref[...] = part

        @pl.when(j > 0)
        def _():
            acc_ref[...] += part

        @pl.when(j == nj - 1)
        def _():
            o_ref[...] = acc_ref[...].astype(o_ref.dtype)

    @pl.when(jnp.logical_not(live) & (j == nj - 1))
    def _():
        o_ref[...] = jnp.zeros(o_ref.shape, o_ref.dtype)


def _ffn_grouped(h, tile_expert, n_tiles, wg, wu, wd, *, tm, tf, src_token=None):
    gather = src_token is not None
    D = h.shape[1]
    R = src_token.shape[0] if gather else h.shape[0]
    F = wg.shape[-1]
    nt_max = R // tm
    nj = F // tf

    def live_tile(i, nt):
        return jnp.minimum(i, jnp.maximum(nt[0] - 1, 0))

    w_specs = [
        pl.BlockSpec((None, D, tf), lambda i, j, te, nt: (te[live_tile(i, nt)], 0, j)),
        pl.BlockSpec((None, D, tf), lambda i, j, te, nt: (te[live_tile(i, nt)], 0, j)),
        pl.BlockSpec((None, tf, D), lambda i, j, te, nt: (te[live_tile(i, nt)], j, 0)),
    ]
    scratch = [pltpu.VMEM((tm, D), F32)]
    if gather:
        assert nj >= 2, "the next tile's rows are fetched during the last of several d_ff steps"
        x_specs = [
            pl.BlockSpec((None, 1, tm),
                         lambda i, j, te, nt: (jnp.minimum(i + (j == nj - 1).astype(jnp.int32), nt_max - 1), 0, 0),
                         memory_space=pltpu.SMEM),
            pl.BlockSpec(memory_space=pl.ANY),
        ]
        x_args = [src_token.reshape(nt_max, 1, tm), h]
        scratch += [pltpu.VMEM((2, tm, D), F32), pltpu.SemaphoreType.DMA((2,))]
    else:
        x_specs = [pl.BlockSpec((tm, D), lambda i, j, te, nt: (live_tile(i, nt), 0))]
        x_args = [h]
    grid_spec = pltpu.PrefetchScalarGridSpec(
        num_scalar_prefetch=2,
        grid=(nt_max, nj),
        in_specs=x_specs + w_specs,
        out_specs=pl.BlockSpec((tm, D), lambda i, j, te, nt: (i, 0)),
        scratch_shapes=scratch,
    )
    return pl.pallas_call(
        functools.partial(_ffn_kernel, gather=gather),
        grid_spec=grid_spec,
        out_shape=jax.ShapeDtypeStruct((R, D), BF16),
        compiler_params=_cparams(("arbitrary", "arbitrary")),
    )(tile_expert, n_tiles, *x_args, wg, wu, wd)


def _combine_kernel(x_ref, g2_ref, *rest, n_terms, weighted, final):
    ys = rest[:n_terms]
    ws = rest[n_terms:2 * n_terms] if weighted else ()
    rest = rest[n_terms + len(ws):]
    if final:
        fg_ref, o_ref = rest
    else:
        (o_ref,) = rest
    f = None
    for k in range(n_terms):
        term = ys[k][...].astype(F32)
        if weighted:
            term = ws[k][...] * term
        f = term if f is None else f + term
    x = x_ref[...] + g2_ref[...] * f
    if final:
        x = (x * lax.rsqrt(jnp.mean(x * x, axis=-1, keepdims=True) + EPS)) * fg_ref[...]
    o_ref[...] = x


def _combine(x, mod, l, ys, ws, final_g, *, tm):
    N, D = x.shape
    r = lambda w: pl.BlockSpec((tm, w), lambda i: (i, 0))
    ins = [r(D), mod.spec(l, 5, tm)] + [r(D)] * len(ys) + [r(1)] * len(ws)
    args = [x, mod.arr, *ys, *ws]
    if final_g is not None:
        ins.append(pl.BlockSpec(final_g.shape, lambda i: (0, 0)))
        args.append(final_g)
    return pl.pallas_call(
        functools.partial(_combine_kernel, n_terms=len(ys), weighted=bool(ws), final=final_g is not None),
        grid=(N // tm,),
        in_specs=ins,
        out_specs=r(D),
        out_shape=jax.ShapeDtypeStruct((N, D), F32),
        compiler_params=_cparams(("arbitrary",)),
    )(*args)


def _route(logits, n_experts, tm):
    N = logits.shape[0]
    top_v, top_i = lax.top_k(logits, TOP_K)
    w = jax.nn.softmax(top_v, axis=-1)
    flat_e = top_i.reshape(-1).astype(jnp.int32)
    n_slots = flat_e.shape[0]
    onehot = (flat_e[:, None] == jnp.arange(n_experts, dtype=jnp.int32)[None, :]).astype(jnp.int32)
    csum = jnp.cumsum(onehot, axis=0)
    counts = csum[-1]
    rank = jnp.sum(onehot * csum, axis=1) - 1
    padded = ((counts + tm - 1) // tm) * tm
    pad_end = jnp.cumsum(padded)
    pad_start = pad_end - padded
    start = jnp.cumsum(counts) - counts
    pos = (jnp.sum(onehot * pad_start[None, :], axis=1) + rank).reshape(N, TOP_K)
    R = n_slots + n_experts * tm
    tile_ids = jnp.arange(R // tm, dtype=jnp.int32)
    tile_expert = jnp.minimum(jnp.searchsorted(pad_end // tm, tile_ids, side="right"), n_experts - 1).astype(jnp.int32)
    n_tiles = (pad_end[-1] // tm).reshape(1).astype(jnp.int32)
    order = jnp.argsort(flat_e, stable=True).astype(jnp.int32)
    row_e = jnp.repeat(tile_expert, tm)
    row_rank = jnp.arange(R, dtype=jnp.int32) - pad_start[row_e]
    src_slot = order[jnp.clip(start[row_e] + row_rank, 0, n_slots - 1)]
    src_token = jnp.where(row_rank < counts[row_e], src_slot // TOP_K, 0)
    return w, src_token, pos, tile_expert, n_tiles


def _prep_weights(p):
    HW = p["w_branch_a"].shape[1]
    w_in = p["w_in"]
    n_f = p["b_fox_f"].shape[-1]
    q_end = 6 * HW
    wf = jnp.pad(w_in[:, :, q_end:q_end + n_f], ((0, 0), (0, 0), (0, LANES - n_f)))
    router = p["moe_router"]
    return dict(
        wqkv=w_in[:, :, :q_end].astype(BF16),
        wf=wf.astype(BF16),
        wgate=w_in[:, :, q_end + n_f:].astype(BF16),
        wa=p["w_branch_a"].astype(BF16), wb=p["w_branch_b"].astype(BF16), wo=p["w_out"].astype(BF16),
        ffn_g=p["ffn_w_gate"].astype(BF16), ffn_u=p["ffn_w_up"].astype(BF16), ffn_d=p["ffn_w_down"].astype(BF16),
        moe_g=p["moe_w_gate"].astype(BF16), moe_u=p["moe_w_up"].astype(BF16), moe_d=p["moe_w_down"].astype(BF16),
        router=jnp.pad(router, ((0, 0), (0, 0), (0, LANES - router.shape[-1]))),
    )


def _trunk(x, mod, p, w, attn_fn, *, tm, tm_moe, tf, emit_bf16, q_scale_a, q_scale_b, chunk):
    depth = p["w_in"].shape[0]
    n_experts = p["moe_router"].shape[-1]
    rows = [[], [], [], [], []]
    for l in range(depth):
        pr = _inproj(x, mod, l, p["norm_attn_g"][l][None], w["wqkv"][l], w["wf"][l], w["wgate"][l],
                     p["b_fox_f"][l][None], emit_bf16=emit_bf16, scale_a=q_scale_a, scale_b=q_scale_b, tm=tm,
                     chunk=chunk)
        qa, ka, va, qb, kb, vb, logf, sga, sgb = pr[:9]
        ya, yb = attn_fn(l, pr)
        x = _post_attn(ya, yb, sga, sgb, x, mod, l, w["wa"][l], w["wb"][l], w["wo"][l], tm=tm)
        final_g = p["final_norm_g"][None] if l == depth - 1 else None
        i = l // 2
        if l % 2 == 0:
            h, _ = _ffn_norm(x, mod, l, p["norm_ffn_g"][l][None], None, tm=tm)
            N = h.shape[0]
            tmd = _pick_tile(N, tm_moe)
            y = _ffn_grouped(h, jnp.zeros((N // tmd,), jnp.int32), jnp.full((1,), N // tmd, jnp.int32),
                             w["ffn_g"][i:i + 1], w["ffn_u"][i:i + 1], w["ffn_d"][i:i + 1], tm=tmd, tf=tf)
            x = _combine(x, mod, l, [y], [], final_g, tm=tm)
        else:
            h, logits = _ffn_norm(x, mod, l, p["norm_ffn_g"][l][None], w["router"][i], tm=tm)
            gate_w, src_token, pos, tile_expert, n_tiles = _route(logits[:, :n_experts], n_experts, tm_moe)
            y = _ffn_grouped(h, tile_expert, n_tiles, w["moe_g"][i], w["moe_u"][i], w["moe_d"][i],
                             tm=tm_moe, tf=tf, src_token=src_token)
            ys = [jnp.take(y, pos[:, k], axis=0, mode="clip") for k in range(TOP_K)]
            ws = [gate_w[:, k:k + 1] for k in range(TOP_K)]
            x = _combine(x, mod, l, ys, ws, final_g, tm=tm)
        for r, n in zip(rows, (ka, va, kb, vb, logf)):
            r.append(n)
    return x, [jnp.stack(r) for r in rows]


def kernel(x_prompt, x_sample, c_prompt, c_sample, cache_diff_k, cache_diff_v, cache_fox_k, cache_fox_v, cache_fox_logf, page_table, w_ada, b_ada, norm_attn_g, norm_ffn_g, w_in, b_fox_f, lambda_q1, lambda_k1, lambda_q2, lambda_k2, subln_g, rel_bias, w_branch_a, w_branch_b, w_out, ffn_w_gate, ffn_w_up, ffn_w_down, moe_router, moe_w_gate, moe_w_up, moe_w_down, final_norm_g):
    p = dict(w_ada=w_ada, b_ada=b_ada, norm_attn_g=norm_attn_g, norm_ffn_g=norm_ffn_g, w_in=w_in,
             b_fox_f=b_fox_f, subln_g=subln_g, rel_bias=rel_bias, w_branch_a=w_branch_a,
             w_branch_b=w_branch_b, w_out=w_out, ffn_w_gate=ffn_w_gate, ffn_w_up=ffn_w_up,
             ffn_w_down=ffn_w_down, moe_router=moe_router, moe_w_gate=moe_w_gate, moe_w_up=moe_w_up,
             moe_w_down=moe_w_down, final_norm_g=final_norm_g)
    B, T, D = x_prompt.shape
    Bs, S, _ = x_sample.shape
    depth, n_phys, page, HA, DK2 = cache_diff_k.shape
    HB, DH = cache_fox_k.shape[3:]
    DK = DK2 // 2
    n_pages = page_table.shape[1]
    past_len = n_pages * page
    W = HA * DK2
    assert DK2 == LANES and 2 * DH == LANES and HB * DH == W and cache_diff_v.shape[-1] == LANES
    G = W // LANES
    d_ff = ffn_w_gate.shape[-1]
    tf = d_ff // 2 if (d_ff // 2) % LANES == 0 else d_ff

    w = _prep_weights(p)
    bp = B + (-B) % SUBLANES
    mod_p, mod_s = _ada_mod(jnp.pad(c_prompt, ((0, bp - B), (0, 0))), c_sample, w_ada, b_ada)
    mod_p = _Mod(mod_p.reshape(-1, 1, D), T, nb=bp)
    mod_s = _Mod(jnp.repeat(mod_s, S, axis=1), 1)

    lam_init = np.array([0.8 - 0.6 * math.exp(-0.3 * l) for l in range(depth)], np.float32)
    lam = (jnp.exp(jnp.sum(lambda_q1 * lambda_k1, axis=-1)) - jnp.exp(jnp.sum(lambda_q2 * lambda_k2, axis=-1))
           + lam_init)

    t = _pick_tile(T, FLASH_TILE)
    key_ = np.arange(t)[:, None]
    qry_ = (np.arange(2 * t) % t)[None, :]
    far_b = np.unique(_t5_bucket_np(np.arange(t + 1, max(T, t + 2))))
    assert far_b.size == 1, "far key blocks must share one bias bucket"
    bkt = np.concatenate([_t5_bucket_np(qry_ - key_), _t5_bucket_np(t + qry_ - key_)], axis=0)
    c_far = rel_bias[int(far_b[0])]
    b01 = (_bias_lookup(rel_bias, bkt) - c_far[:, None, None]) * LOG2E
    b0, b1 = b01[:, :t], b01[:, t:]
    kpos = np.arange(past_len + page)[None, :]
    qpos = past_len + np.tile(np.arange(S), 2)[:, None]
    bias_s = _bias_lookup(rel_bias, _t5_bucket_np(qpos - kpos))

    def scal(l):
        return jnp.concatenate([c_far, lam[l][None], jnp.full((1,), 1.0 - float(lam_init[l]), F32)]).astype(F32)

    def attn_prompt(l, pr):
        qa, qb, logf = pr[0], pr[3], pr[6]
        kab, vab, kbb, vbb = pr[9:13]
        r3 = lambda a: a.reshape(B, T, W)
        rt = lambda a: a.reshape(B, T // t, W, t)
        ya = _flash_prompt("diff", rt(qa), r3(kab), rt(vab), scal(l), (b0, b1, subln_g[l][:, None]), t=t)
        cum = jnp.cumsum(logf.reshape(B, T, HB), axis=1)
        kx, qx = _decay_columns(cum.reshape(B, T, G, 2) * LOG2E)
        yb = _flash_prompt("fox", rt(qb), r3(kbb), rt(vbb), jnp.zeros((1,), F32), (kx, qx), t=t)
        return ya.reshape(B * T, W), yb.reshape(B * T, W)

    y_p, rows_p = _trunk(x_prompt.reshape(B * T, D), mod_p, p, w, attn_prompt,
                         tm=_pick_tile(t, 256), tm_moe=_pick_tile(T, 512), tf=tf, emit_bf16=True,
                         q_scale_a=DK ** -0.5 * LOG2E, q_scale_b=DH ** -0.5 * LOG2E, chunk=t)

    diff_view = lambda c: c.reshape(depth, n_phys, page * HA, DK2)
    fox_view = lambda c: c.transpose(0, 1, 3, 4, 2).reshape(depth, n_phys, HB * DH, page)
    cdk, cdv, cfk, cfv = diff_view(cache_diff_k), diff_view(cache_diff_v), fox_view(cache_fox_k), fox_view(cache_fox_v)

    def attn_sample(l, pr):
        qa, ka, va, qb, kb, vb, logf = pr[:7]
        ya = _attn_sample("diff", qa, ka, va, scal(l), (bias_s, subln_g[l][None]), cdk, cdv, l, page_table)
        past_lf = cache_fox_logf[l][page_table].reshape(Bs, past_len, HB)
        cum = jnp.cumsum(jnp.concatenate([past_lf, logf.reshape(Bs, S, HB)], axis=1), axis=1)
        cq = cum[:, past_len:]
        ck = jnp.pad(cum, ((0, 0), (0, page - S), (0, 0))).transpose(0, 2, 1)
        yb = _attn_sample("fox", qb, kb, vb, jnp.zeros((1,), F32), (cq, ck), cfk, cfv, l, page_table)
        return ya, yb

    Ns = Bs * S
    y_s, rows_s = _trunk(x_sample.reshape(Ns, D), mod_s, p, w, attn_sample,
                         tm=_pick_tile(Ns, 256), tm_moe=_pick_tile(Ns, 256), tf=tf, emit_bf16=False,
                         q_scale_a=DK ** -0.5, q_scale_b=DH ** -0.5, chunk=_pick_tile(Ns, 256))

    def shape_rows(rows, b, s):
        dk_, dv_, fk_, fv_, lf_ = rows
        return (dk_.reshape(depth, b, s, HA, DK2), dv_.reshape(depth, b, s, HA, LANES),
                fk_.reshape(depth, b, s, HB, DH), fv_.reshape(depth, b, s, HB, DH), lf_.reshape(depth, b, s, HB))

    return (y_p.reshape(B, T, D), y_s.reshape(Bs, S, D)) + shape_rows(rows_p, B, T) + shape_rows(rows_s, Bs, S)
```

```python
import functools
import math

import jax
import jax.numpy as jnp
import numpy as np
from jax import lax
from jax.experimental import pallas as pl
from jax.experimental.pallas import tpu as pltpu

F32 = jnp.float32
BF16 = jnp.bfloat16

EPS = 1e-6
NEG_INF = -1e30
NUM_BUCKETS = 32
MAX_DISTANCE = 128
TOP_K = 2
LOG2E = 1.4426950408889634
FLASH_TILE = 256

LANES = 128
SUBLANES = 8
VMEM_LIMIT = 56 * 1024 * 1024


def _cparams(sem):
    return pltpu.CompilerParams(dimension_semantics=sem, vmem_limit_bytes=VMEM_LIMIT)


def _pick_tile(n, pref):
    t = min(n, pref)
    while n % t:
        t //= 2
    return t


def _t5_bucket_np(rel):
    n = np.maximum(rel, 0)
    max_exact = NUM_BUCKETS // 2
    nf = np.maximum(n, 1).astype(np.float32)
    large = max_exact + (np.log(nf / np.float32(max_exact)) / np.float32(math.log(MAX_DISTANCE / max_exact))
                         * np.float32(NUM_BUCKETS - max_exact)).astype(np.int32)
    large = np.minimum(large, NUM_BUCKETS - 1)
    return np.where(n < max_exact, n, large).astype(np.int32)


def _ada_kernel(cp_ref, cs_ref, w_ref, b_ref, op_ref, os_ref):
    w = w_ref[...].astype(BF16)
    b = b_ref[...]
    op_ref[...] = jnp.dot(cp_ref[...].astype(BF16), w, preferred_element_type=F32) + b
    os_ref[...] = jnp.dot(cs_ref[...].astype(BF16), w, preferred_element_type=F32) + b


def _ada_mod(c_p, c_s, w_ada, b_ada):
    L, D, D6 = w_ada.shape
    nj = D6 // D
    bp, bs = c_p.shape[0], c_s.shape[0]
    b3 = b_ada.reshape(L, 1, D6)
    return pl.pallas_call(
        _ada_kernel,
        grid=(L, nj),
        in_specs=[
            pl.BlockSpec((bp, D), lambda l, j: (0, 0)),
            pl.BlockSpec((bs, D), lambda l, j: (0, 0)),
            pl.BlockSpec((None, D, D), lambda l, j: (l, 0, j)),
            pl.BlockSpec((None, 1, D), lambda l, j: (l, 0, j)),
        ],
        out_specs=[
            pl.BlockSpec((None, bp, D), lambda l, j: (l * nj + j, 0, 0)),
            pl.BlockSpec((None, bs, D), lambda l, j: (l * nj + j, 0, 0)),
        ],
        out_shape=[jax.ShapeDtypeStruct((L * nj, bp, D), F32),
                   jax.ShapeDtypeStruct((L * nj, bs, D), F32)],
        compiler_params=_cparams(("arbitrary", "arbitrary")),
    )(c_p, c_s, w_ada, b3)


class _Mod:
    def __init__(self, arr, rows_per_mod_row, nb=1):
        self.arr = arr
        self.rows_per = rows_per_mod_row
        self.nb = nb

    def spec(self, l, j, tm):
        D = self.arr.shape[-1]
        idx = l * 6 + j
        if self.rows_per == 1:
            return pl.BlockSpec((None, tm, D), lambda i, *_: (idx, i, 0))
        assert self.rows_per % tm == 0
        per = self.rows_per // tm
        nb = self.nb
        return pl.BlockSpec((None, 1, D), lambda i, *_: (idx * nb + i // per, 0, 0))


def _rmsnorm_mod(x, g, sc, sh):
    y = x * lax.rsqrt(jnp.mean(x * x, axis=-1, keepdims=True) + EPS)
    return (y * g) * (1.0 + sc) + sh


def _inproj_kernel(x_ref, sc_ref, sh_ref, g_ref, wqkv_ref, wf_ref, wg_ref, bf_ref, *outs,
                   hw, n_f, emit_bf16, scale_a, scale_b):
    if emit_bf16:
        (qa_ref, ka_ref, va_ref, qb_ref, kb_ref, vb_ref, lf_ref, sga_ref, sgb_ref,
         kab_ref, vab_ref, kbb_ref, vbb_ref) = outs
    else:
        qa_ref, ka_ref, va_ref, qb_ref, kb_ref, vb_ref, lf_ref, sga_ref, sgb_ref = outs
        kab_ref = vab_ref = kbb_ref = vbb_ref = None
    h = _rmsnorm_mod(x_ref[...], g_ref[...], sc_ref[...], sh_ref[...]).astype(BF16)

    def proj(c):
        return jnp.dot(h, wqkv_ref[:, c * hw:(c + 1) * hw], preferred_element_type=F32)

    for c, o, scale in ((0, qa_ref, scale_a), (3, qb_ref, scale_b)):
        y = proj(c) * scale
        o[...] = (y.T if emit_bf16 else y).astype(o.dtype)
    for c, o32, o16, transposed in ((1, ka_ref, kab_ref, False), (2, va_ref, vab_ref, True),
                                    (4, kb_ref, kbb_ref, False), (5, vb_ref, vbb_ref, True)):
        y = proj(c)
        o32[...] = y
        if o16 is not None:
            o16[...] = (y.T if transposed else y).astype(BF16)
    z = jnp.dot(h, wf_ref[...], preferred_element_type=F32)[:, :n_f] + bf_ref[...]
    lf_ref[...] = jnp.minimum(z, 0.0) - jnp.log(1.0 + jnp.exp(-jnp.abs(z)))
    D = sga_ref.shape[-1]
    ga = jnp.dot(h, wg_ref[:, :D], preferred_element_type=F32)
    sga_ref[...] = jax.nn.sigmoid(ga).astype(sga_ref.dtype)
    gb = jnp.dot(h, wg_ref[:, D:], preferred_element_type=F32)
    sgb_ref[...] = jax.nn.sigmoid(gb).astype(sgb_ref.dtype)


def _inproj(x, mod, l, g_attn, wqkv, wf, wg, b_f, *, emit_bf16, scale_a, scale_b, tm, chunk):
    N, D = x.shape
    hw = wqkv.shape[1] // 6
    n_f = b_f.shape[-1]
    row = lambda w, dt: (pl.BlockSpec((tm, w), lambda i: (i, 0)), jax.ShapeDtypeStruct((N, w), dt))
    per = chunk // tm
    vt = (pl.BlockSpec((None, hw, tm), lambda i: (i // per, 0, i % per)),
          jax.ShapeDtypeStruct((N // chunk, hw, chunk), BF16))
    qo = vt if emit_bf16 else row(hw, F32)
    outs = [qo, row(hw, F32), row(hw, F32), qo, row(hw, F32), row(hw, F32),
            row(n_f, F32), row(D, BF16), row(D, BF16)]
    if emit_bf16:
        outs += [row(hw, BF16), vt, row(hw, BF16), vt]
    full = lambda a: pl.BlockSpec(a.shape, lambda i: (0,) * a.ndim)
    return pl.pallas_call(
        functools.partial(_inproj_kernel, hw=hw, n_f=n_f, emit_bf16=emit_bf16,
                          scale_a=scale_a, scale_b=scale_b),
        grid=(N // tm,),
        in_specs=[pl.BlockSpec((tm, D), lambda i: (i, 0)), mod.spec(l, 1, tm), mod.spec(l, 0, tm),
                  full(g_attn), full(wqkv), full(wf), full(wg), full(b_f)],
        out_specs=[o[0] for o in outs],
        out_shape=[o[1] for o in outs],
        compiler_params=_cparams(("arbitrary",)),
    )(x, mod.arr, mod.arr, g_attn, wqkv, wf, wg, b_f)


def _two_map_queries(q):
    lane = lax.broadcasted_iota(jnp.int32, q.shape, 1)
    half = q.shape[-1] // 2
    zero = jnp.zeros_like(q)
    return jnp.where(lane < half, q, zero), jnp.where(lane >= half, q, zero)


def _finish_diff(o1, o2, lam, out_scale, g):
    y = o1 - lam * o2
    y = y * lax.rsqrt(jnp.mean(y * y, axis=-1, keepdims=True) + EPS)
    return (y * g) * out_scale


def _finish_fox(o1, o2):
    lane = lax.broadcasted_iota(jnp.int32, o1.shape, 1)
    return jnp.where(lane < o1.shape[-1] // 2, o1, o2)


def _split3_bf16(c):
    def trunc(x):
        bits = lax.bitcast_convert_type(x, jnp.uint32) & jnp.uint32(0xFFFF0000)
        return lax.bitcast_convert_type(bits, F32)
    hi = trunc(c)
    r = c - hi
    mid = trunc(r)
    return hi.astype(BF16), mid.astype(BF16), (r - mid).astype(BF16)


def _decay_columns(cum):
    B, T, G, _ = cum.shape
    parts = jnp.stack(_split3_bf16(cum), axis=-1)
    one = jnp.ones((B, T, G, 3), BF16)
    zero = jnp.zeros((B, T, G, 6), BF16)
    kx = jnp.concatenate([-parts[..., 0, :], one, -parts[..., 1, :], one], axis=-1)
    q0 = jnp.concatenate([one, parts[..., 0, :], zero], axis=-1)
    q1 = jnp.concatenate([zero, one, parts[..., 1, :]], axis=-1)
    padl = lambda a: jnp.pad(a, [(0, 0)] * (a.ndim - 1) + [(0, LANES - a.shape[-1])])
    kx = padl(kx).transpose(0, 2, 1, 3)
    qx = jnp.stack([padl(q0), padl(q1)], axis=0).transpose(1, 3, 0, 4, 2)
    return kx, qx


def _flash_prompt_kernel(scal_ref, q_ref, k_ref, vt_ref, *rest, mode, t):
    if mode == "diff":
        b0_ref, b1_ref, g_ref = rest[:3]
    else:
        kx_ref, qx_ref = rest[:2]
    o_ref, m_sc, l_sc, acc_sc, sa_sc, sb_sc, pb_sc, ab_sc = rest[-8:]
    qi = pl.program_id(2)
    qt = q_ref[...]
    feat = lax.broadcasted_iota(jnp.int32, qt.shape, 0)
    zero = jnp.zeros_like(qt)
    qq = jnp.concatenate([jnp.where(feat < LANES // 2, qt, zero),
                          jnp.where(feat >= LANES // 2, qt, zero)], axis=1)
    if mode == "fox":
        qq = jnp.concatenate([qq, jnp.concatenate([qx_ref[0], qx_ref[1]], axis=1)], axis=0)
    m_sc[...] = jnp.full(m_sc.shape, NEG_INF, F32)
    l_sc[...] = jnp.zeros(l_sc.shape, F32)
    acc_sc[...] = jnp.zeros(acc_sc.shape, F32)

    def scores(kj, kind):
        ks = pl.multiple_of(kj * t, t)
        k = k_ref[pl.ds(ks, t), :]
        if mode == "fox":
            k = jnp.concatenate([k, kx_ref[pl.ds(ks, t), :]], axis=1)
        s = jnp.dot(k, qq, preferred_element_type=F32)
        if kind == "diag":
            key = lax.broadcasted_iota(jnp.int32, (t, 2 * t), 0)
            qry = lax.broadcasted_iota(jnp.int32, (t, 2 * t), 1)
            qry = jnp.where(qry >= t, qry - t, qry)
            visible = key <= qry
        if mode == "diff":
            if kind == "near":
                s = s + b1_ref[...]
            elif kind == "diag":
                s = jnp.where(visible, s + b0_ref[...], NEG_INF)
        elif kind == "diag":
            s = jnp.where(visible, s, NEG_INF)
        return s

    def softmax_update(s):
        m_prev = m_sc[...]
        m_new = jnp.maximum(m_prev, jnp.max(s, axis=0, keepdims=True))
        alpha = jnp.exp2(m_prev - m_new)
        p = jnp.exp2(s - m_new)
        l_sc[...] = alpha * l_sc[...] + jnp.sum(p, axis=0, keepdims=True)
        m_sc[...] = m_new
        return p.astype(BF16), alpha

    def pv_update(kj, p, alpha):
        acc_sc[...] = alpha * acc_sc[...] + jnp.dot(vt_ref[kj], p, preferred_element_type=F32)

    def step(kj, kind):
        p, alpha = softmax_update(scores(kj, kind))
        pv_update(kj, p, alpha)

    n_far = jnp.maximum(qi - 1 if mode == "diff" else qi, 0)
    n_pairs = lax.shift_right_logical(n_far, 1)
    pb_sc[...] = jnp.zeros(pb_sc.shape, BF16)
    ab_sc[...] = jnp.ones(ab_sc.shape, F32)

    @pl.when(n_pairs > 0)
    def _():
        sa_sc[...] = scores(0, "far")

    def pair_body(i, c):
        a = 2 * i
        sb_sc[...] = scores(a + 1, "far")
        pv_update(jnp.maximum(a - 1, 0), pb_sc[...], ab_sc[...])
        p_a, alpha_a = softmax_update(sa_sc[...])
        sa_sc[...] = scores(jnp.minimum(a + 2, 2 * (n_pairs - 1)), "far")
        p_b, alpha_b = softmax_update(sb_sc[...])
        pv_update(a, p_a, alpha_a)
        pb_sc[...] = p_b
        ab_sc[...] = alpha_b
        return c

    lax.fori_loop(0, n_pairs, pair_body, 0)
    pv_update(jnp.maximum(2 * n_pairs - 1, 0), pb_sc[...], ab_sc[...])

    def two_steps(kj_a, kind_a, kj_b, kind_b):
        s_a = scores(kj_a, kind_a)
        s_b = scores(kj_b, kind_b)
        p_a, alpha_a = softmax_update(s_a)
        p_b, alpha_b = softmax_update(s_b)
        pv_update(kj_a, p_a, alpha_a)
        pv_update(kj_b, p_b, alpha_b)

    odd_far = n_far - 2 * n_pairs == 1
    if mode == "diff":
        @pl.when(odd_far)
        def _():
            step(n_far - 1, "far")

        @pl.when(qi >= 1)
        def _():
            two_steps(qi - 1, "near", qi, "diag")

        @pl.when(qi == 0)
        def _():
            step(qi, "diag")
    else:
        @pl.when(odd_far)
        def _():
            two_steps(n_far - 1, "far", qi, "diag")

        @pl.when(jnp.logical_not(odd_far))
        def _():
            step(qi, "diag")

    o = acc_sc[...] / l_sc[...]
    o1, o2 = o[:, :t], o[:, t:]
    if mode == "diff":
        nh = scal_ref.shape[0] - 2
        y = o1 - scal_ref[nh] * o2
        y = y * lax.rsqrt(jnp.mean(y * y, axis=0, keepdims=True) + EPS)
        y = (y * g_ref[...]) * scal_ref[nh + 1]
    else:
        feat = lax.broadcasted_iota(jnp.int32, o1.shape, 0)
        y = jnp.where(feat < LANES // 2, o1, o2)
    o_ref[...] = y.T.astype(o_ref.dtype)


def _flash_prompt(mode, qt, k, vt, scal, extra, *, t):
    B, T, W = k.shape
    G = W // LANES
    nq = T // t
    qspec = pl.BlockSpec((None, None, LANES, t), lambda b, g, i: (b, i, g, 0))
    ospec = pl.BlockSpec((None, t, LANES), lambda b, g, i: (b, i, g))
    kspec = pl.BlockSpec((None, T, LANES), lambda b, g, i: (b, 0, g))
    vspec = pl.BlockSpec((None, nq, LANES, t), lambda b, g, i: (b, 0, g, 0))
    if mode == "diff":
        especs = [pl.BlockSpec((None, t, 2 * t), lambda b, g, i: (g, 0, 0)),
                  pl.BlockSpec((None, t, 2 * t), lambda b, g, i: (g, 0, 0)),
                  pl.BlockSpec((LANES, 1), lambda b, g, i: (0, 0))]
    else:
        especs = [pl.BlockSpec((None, None, T, LANES), lambda b, g, i: (b, g, 0, 0)),
                  pl.BlockSpec((None, None, 2, LANES, t), lambda b, g, i: (b, g, 0, 0, i))]
    return pl.pallas_call(
        functools.partial(_flash_prompt_kernel, mode=mode, t=t),
        grid=(B, G, nq),
        in_specs=[pl.BlockSpec(memory_space=pltpu.SMEM), qspec, kspec, vspec] + especs,
        out_specs=ospec,
        out_shape=jax.ShapeDtypeStruct((B, T, W), BF16),
        scratch_shapes=[pltpu.VMEM((1, 2 * t), F32), pltpu.VMEM((1, 2 * t), F32),
                        pltpu.VMEM((LANES, 2 * t), F32),
                        pltpu.VMEM((t, 2 * t), F32), pltpu.VMEM((t, 2 * t), F32),
                        pltpu.VMEM((t, 2 * t), BF16), pltpu.VMEM((1, 2 * t), F32)],
        compiler_params=_cparams(("arbitrary", "arbitrary", "arbitrary")),
    )(scal, qt, k, vt, *extra)


def _attn_sample_kernel(pt_ref, scal_ref, q_ref, kn_ref, vn_ref, *rest, mode, n_pages, page, n_groups):
    del pt_ref
    if mode == "diff":
        bias_ref, g_ref = rest[:2]
        rest = rest[2:]
    else:
        cq_ref, ck_ref = rest[:2]
        rest = rest[2:]
    k_refs = rest[:n_pages]
    v_refs = rest[n_pages:2 * n_pages]
    o_ref = rest[2 * n_pages]
    S = q_ref.shape[0]
    nt_dims = (((1,), (1,)), ((), ()))
    rows = 2 * S
    nblk = n_pages + 1
    r_i = lax.broadcasted_iota(jnp.int32, (rows, page), 0)
    c_i = lax.broadcasted_iota(jnp.int32, (rows, page), 1)
    tok = jnp.where(r_i >= S, r_i - S, r_i)
    new_ok = (c_i < S) & (c_i <= tok)
    top = lax.broadcasted_iota(jnp.int32, (rows, 1), 0) < S
    pad = jnp.zeros((page - S, LANES), F32)
    for g in range(n_groups):
        ls = slice(g * LANES, (g + 1) * LANES)
        q1, q2 = _two_map_queries(q_ref[:, ls])
        q2m = jnp.concatenate([q1, q2], axis=0).astype(BF16)
        kn = jnp.concatenate([kn_ref[:, ls], pad], axis=0).astype(BF16)
        vn = jnp.concatenate([vn_ref[:, ls], pad], axis=0).astype(BF16)
        s_blocks = []
        for p in range(nblk):
            if p == n_pages:
                s = lax.dot_general(q2m, kn, nt_dims, preferred_element_type=F32)
            elif mode == "diff":
                kp = k_refs[p][pl.ds(g, page, stride=n_groups), :].astype(BF16)
                s = lax.dot_general(q2m, kp, nt_dims, preferred_element_type=F32)
            else:
                kp = k_refs[p][g * LANES:(g + 1) * LANES, :].astype(BF16)
                s = jnp.dot(q2m, kp, preferred_element_type=F32)
            s_blocks.append(s)
        if mode == "diff":
            add = [bias_ref[g, :, p * page:(p + 1) * page] for p in range(nblk)]
        else:
            cq = jnp.where(top, jnp.concatenate([cq_ref[:, 2 * g:2 * g + 1]] * 2, axis=0),
                           jnp.concatenate([cq_ref[:, 2 * g + 1:2 * g + 2]] * 2, axis=0))
            add = []
            for p in range(nblk):
                cs = slice(p * page, (p + 1) * page)
                ck = jnp.where(top, ck_ref[2 * g:2 * g + 1, cs], ck_ref[2 * g + 1:2 * g + 2, cs])
                add.append(cq - ck)
        s_blocks = [s + a for s, a in zip(s_blocks, add)]
        s_blocks[n_pages] = jnp.where(new_ok, s_blocks[n_pages], NEG_INF)
        m = s_blocks[0].max(axis=1, keepdims=True)
        for s in s_blocks[1:]:
            m = jnp.maximum(m, s.max(axis=1, keepdims=True))
        l = jnp.zeros((rows, 1), F32)
        acc = jnp.zeros((rows, LANES), F32)
        for p in range(nblk):
            pr = jnp.exp(s_blocks[p] - m)
            l = l + pr.sum(axis=1, keepdims=True)
            prb = pr.astype(BF16)
            if p == n_pages:
                acc = acc + jnp.dot(prb, vn, preferred_element_type=F32)
            elif mode == "diff":
                vp = v_refs[p][pl.ds(g, page, stride=n_groups), :].astype(BF16)
                acc = acc + jnp.dot(prb, vp, preferred_element_type=F32)
            else:
                vp = v_refs[p][g * LANES:(g + 1) * LANES, :].astype(BF16)
                acc = acc + lax.dot_general(prb, vp, nt_dims, preferred_element_type=F32)
        o = acc / l
        o1, o2 = o[:S], o[S:]
        if mode == "diff":
            nh = scal_ref.shape[0] - 2
            y = _finish_diff(o1, o2, scal_ref[nh], scal_ref[nh + 1], g_ref[...])
        else:
            y = _finish_fox(o1, o2)
        o_ref[:, ls] = y.astype(o_ref.dtype)


def _attn_sample(mode, q, kn, vn, scal, extra, cache_k, cache_v, l, page_table):
    Bs, n_pages = page_table.shape
    W = q.shape[1]
    S = q.shape[0] // Bs
    n_groups = W // LANES
    page = cache_k.shape[2] // n_groups
    assert page == LANES
    rowspec = pl.BlockSpec((S, W), lambda b, pt: (b, 0))
    if mode == "diff":
        bias, sg = extra
        especs = [pl.BlockSpec(bias.shape, lambda b, pt: (0, 0, 0)),
                  pl.BlockSpec((1, LANES), lambda b, pt: (0, 0))]
    else:
        cq, ck = extra
        especs = [pl.BlockSpec((None,) + cq.shape[1:], lambda b, pt: (b, 0, 0)),
                  pl.BlockSpec((None,) + ck.shape[1:], lambda b, pt: (b, 0, 0))]
    pspecs = [pl.BlockSpec((None, None, page * n_groups, LANES), lambda b, pt, p=p: (l, pt[b, p], 0, 0))
              for p in range(n_pages)]
    grid_spec = pltpu.PrefetchScalarGridSpec(
        num_scalar_prefetch=1,
        grid=(Bs,),
        in_specs=[pl.BlockSpec(memory_space=pltpu.SMEM), rowspec, rowspec, rowspec] + especs + pspecs + pspecs,
        out_specs=rowspec,
    )
    return pl.pallas_call(
        functools.partial(_attn_sample_kernel, mode=mode, n_pages=n_pages, page=page, n_groups=n_groups),
        grid_spec=grid_spec,
        out_shape=jax.ShapeDtypeStruct(q.shape, F32),
        compiler_params=_cparams(("arbitrary",)),
    )(page_table, scal, q, kn, vn, *extra, *([cache_k] * n_pages), *([cache_v] * n_pages))


def _bias_kernel(tab_ref, bkt_ref, o_ref, *, n_buckets):
    h = pl.program_id(0)
    bkt = bkt_ref[...]
    acc = jnp.zeros(bkt.shape, F32)
    for b in range(n_buckets):
        acc = jnp.where(bkt == b, tab_ref[b, h], acc)
    o_ref[...] = acc


def _bias_lookup(rel_bias, buckets):
    nb, H = rel_bias.shape
    R, C = buckets.shape
    return pl.pallas_call(
        functools.partial(_bias_kernel, n_buckets=nb),
        grid=(H,),
        in_specs=[pl.BlockSpec(memory_space=pltpu.SMEM), pl.BlockSpec((R, C), lambda h: (0, 0))],
        out_specs=pl.BlockSpec((None, R, C), lambda h: (h, 0, 0)),
        out_shape=jax.ShapeDtypeStruct((H, R, C), F32),
        compiler_params=_cparams(("arbitrary",)),
    )(rel_bias, jnp.asarray(buckets))


def _post_attn_kernel(ya_ref, yb_ref, sga_ref, sgb_ref, x_ref, g1_ref, wa_ref, wb_ref, wo_ref, o_ref):
    a = jnp.dot(ya_ref[...].astype(BF16), wa_ref[...], preferred_element_type=F32)
    b = jnp.dot(yb_ref[...].astype(BF16), wb_ref[...], preferred_element_type=F32)
    merged = sga_ref[...].astype(F32) * a + sgb_ref[...].astype(F32) * b
    out = jnp.dot(merged.astype(BF16), wo_ref[...], preferred_element_type=F32)
    o_ref[...] = x_ref[...] + g1_ref[...] * out


def _post_attn(ya, yb, sga, sgb, x, mod, l, wa, wb, wo, *, tm):
    N, D = x.shape
    hw = ya.shape[1]
    full = lambda a: pl.BlockSpec(a.shape, lambda i: (0,) * a.ndim)
    r = lambda w: pl.BlockSpec((tm, w), lambda i: (i, 0))
    return pl.pallas_call(
        _post_attn_kernel,
        grid=(N // tm,),
        in_specs=[r(hw), r(hw), r(D), r(D), r(D), mod.spec(l, 2, tm), full(wa), full(wb), full(wo)],
        out_specs=r(D),
        out_shape=jax.ShapeDtypeStruct((N, D), F32),
        compiler_params=_cparams(("arbitrary",)),
    )(ya, yb, sga, sgb, x, mod.arr, wa, wb, wo)


def _ffn_norm_kernel(x_ref, sc_ref, sh_ref, g_ref, *rest, with_router):
    h = _rmsnorm_mod(x_ref[...], g_ref[...], sc_ref[...], sh_ref[...])
    hb = h.astype(BF16)
    if with_router:
        r_ref, h_ref, lg_ref = rest
        h_lo = (h - hb.astype(F32)).astype(BF16)
        r = r_ref[...]
        r_hi = r.astype(BF16)
        r_lo = (r - r_hi.astype(F32)).astype(BF16)
        lg_ref[...] = (jnp.dot(hb, r_hi, preferred_element_type=F32)
                       + jnp.dot(h_lo, r_hi, preferred_element_type=F32)
                       + jnp.dot(hb, r_lo, preferred_element_type=F32))
    else:
        (h_ref,) = rest
    h_ref[...] = h.astype(h_ref.dtype)


def _ffn_norm(x, mod, l, g_ffn, router_pad, *, tm):
    N, D = x.shape
    full = lambda a: pl.BlockSpec(a.shape, lambda i: (0,) * a.ndim)
    r = lambda w: pl.BlockSpec((tm, w), lambda i: (i, 0))
    ins = [r(D), mod.spec(l, 4, tm), mod.spec(l, 3, tm), full(g_ffn)]
    args = [x, mod.arr, mod.arr, g_ffn]
    outs = [(r(D), jax.ShapeDtypeStruct((N, D), BF16 if router_pad is None else F32))]
    if router_pad is not None:
        ins.append(full(router_pad))
        args.append(router_pad)
        outs.append((r(LANES), jax.ShapeDtypeStruct((N, LANES), F32)))
    res = pl.pallas_call(
        functools.partial(_ffn_norm_kernel, with_router=router_pad is not None),
        grid=(N // tm,),
        in_specs=ins,
        out_specs=[o[0] for o in outs],
        out_shape=[o[1] for o in outs],
        compiler_params=_cparams(("arbitrary",)),
    )(*args)
    return res if router_pad is not None else (res[0], None)


def _ffn_kernel(te_ref, nt_ref, *rest, gather):
    del te_ref
    i = pl.program_id(0)
    j = pl.program_id(1)
    nj = pl.num_programs(1)
    live = i < nt_ref[0]
    if gather:
        src_ref, h_hbm, wg_ref, wu_ref, wd_ref, o_ref, acc_ref, xbuf, sem = rest
        tm = xbuf.shape[1]
        slot = lax.rem(i, 2)

        def row_copy(r, s):
            return pltpu.make_async_copy(h_hbm.at[pl.ds(src_ref[0, r], 1)], xbuf.at[s, pl.ds(r, 1)], sem.at[s])

        def start_rows(s):
            def body(r, c):
                row_copy(r, s).start()
                return c
            lax.fori_loop(0, tm, body, 0, unroll=8)

        def wait_rows(s):
            for r in range(tm):
                pltpu.make_async_copy(h_hbm.at[pl.ds(0, 1)], xbuf.at[s, pl.ds(r, 1)], sem.at[s]).wait()

        @pl.when(live & (j == 0))
        def _():
            @pl.when(i == 0)
            def _():
                start_rows(slot)
            wait_rows(slot)

        @pl.when((j == nj - 1) & (i + 1 < nt_ref[0]))
        def _():
            start_rows(1 - slot)
    else:
        h_ref, wg_ref, wu_ref, wd_ref, o_ref, acc_ref = rest

    @pl.when(live)
    def _():
        h = xbuf[slot].astype(BF16) if gather else h_ref[...]
        gt = jnp.dot(h, wg_ref[...], preferred_element_type=F32)
        up = jnp.dot(h, wu_ref[...], preferred_element_type=F32)
        a = (jax.nn.silu(gt) * up).astype(BF16)
        part = jnp.dot(a, wd_ref[...], preferred_element_type=F32)

        @pl.when(j == 0)
        def _():
            acc_ref[...] = part

        @pl.when(j > 0)
        def _():
            acc_ref[...] += part

        @pl.when(j == nj - 1)
        def _():
            o_ref[...] = acc_ref[...].astype(o_ref.dtype)

    @pl.when(jnp.logical_not(live) & (j == nj - 1))
    def _():
        o_ref[...] = jnp.zeros(o_ref.shape, o_ref.dtype)


def _ffn_grouped(h, tile_expert, n_tiles, wg, wu, wd, *, tm, tf, src_token=None):
    gather = src_token is not None
    D = h.shape[1]
    R = src_token.shape[0] if gather else h.shape[0]
    F = wg.shape[-1]
    nt_max = R // tm
    nj = F // tf

    def live_tile(i, nt):
        return jnp.minimum(i, jnp.maximum(nt[0] - 1, 0))

    w_specs = [
        pl.BlockSpec((None, D, tf), lambda i, j, te, nt: (te[live_tile(i, nt)], 0, j)),
        pl.BlockSpec((None, D, tf), lambda i, j, te, nt: (te[live_tile(i, nt)], 0, j)),
        pl.BlockSpec((None, tf, D), lambda i, j, te, nt: (te[live_tile(i, nt)], j, 0)),
    ]
    scratch = [pltpu.VMEM((tm, D), F32)]
    if gather:
        assert nj >= 2, "the next tile's rows are fetched during the last of several d_ff steps"
        x_specs = [
            pl.BlockSpec((None, 1, tm),
                         lambda i, j, te, nt: (jnp.minimum(i + (j == nj - 1).astype(jnp.int32), nt_max - 1), 0, 0),
                         memory_space=pltpu.SMEM),
            pl.BlockSpec(memory_space=pl.ANY),
        ]
        x_args = [src_token.reshape(nt_max, 1, tm), h]
        scratch += [pltpu.VMEM((2, tm, D), F32), pltpu.SemaphoreType.DMA((2,))]
    else:
        x_specs = [pl.BlockSpec((tm, D), lambda i, j, te, nt: (live_tile(i, nt), 0))]
        x_args = [h]
    grid_spec = pltpu.PrefetchScalarGridSpec(
        num_scalar_prefetch=2,
        grid=(nt_max, nj),
        in_specs=x_specs + w_specs,
        out_specs=pl.BlockSpec((tm, D), lambda i, j, te, nt: (i, 0)),
        scratch_shapes=scratch,
    )
    return pl.pallas_call(
        functools.partial(_ffn_kernel, gather=gather),
        grid_spec=grid_spec,
        out_shape=jax.ShapeDtypeStruct((R, D), BF16),
        compiler_params=_cparams(("arbitrary", "arbitrary")),
    )(tile_expert, n_tiles, *x_args, wg, wu, wd)


def _combine_kernel(x_ref, g2_ref, *rest, n_terms, weighted, final):
    ys = rest[:n_terms]
    ws = rest[n_terms:2 * n_terms] if weighted else ()
    rest = rest[n_terms + len(ws):]
    if final:
        fg_ref, o_ref = rest
    else:
        (o_ref,) = rest
    f = None
    for k in range(n_terms):
        term = ys[k][...].astype(F32)
        if weighted:
            term = ws[k][...] * term
        f = term if f is None else f + term
    x = x_ref[...] + g2_ref[...] * f
    if final:
        x = (x * lax.rsqrt(jnp.mean(x * x, axis=-1, keepdims=True) + EPS)) * fg_ref[...]
    o_ref[...] = x


def _combine(x, mod, l, ys, ws, final_g, *, tm):
    N, D = x.shape
    r = lambda w: pl.BlockSpec((tm, w), lambda i: (i, 0))
    ins = [r(D), mod.spec(l, 5, tm)] + [r(D)] * len(ys) + [r(1)] * len(ws)
    args = [x, mod.arr, *ys, *ws]
    if final_g is not None:
        ins.append(pl.BlockSpec(final_g.shape, lambda i: (0, 0)))
        args.append(final_g)
    return pl.pallas_call(
        functools.partial(_combine_kernel, n_terms=len(ys), weighted=bool(ws), final=final_g is not None),
        grid=(N // tm,),
        in_specs=ins,
        out_specs=r(D),
        out_shape=jax.ShapeDtypeStruct((N, D), F32),
        compiler_params=_cparams(("arbitrary",)),
    )(*args)


def _route(logits, n_experts, tm):
    N = logits.shape[0]
    top_v, top_i = lax.top_k(logits, TOP_K)
    w = jax.nn.softmax(top_v, axis=-1)
    flat_e = top_i.reshape(-1).astype(jnp.int32)
    n_slots = flat_e.shape[0]
    onehot = (flat_e[:, None] == jnp.arange(n_experts, dtype=jnp.int32)[None, :]).astype(jnp.int32)
    csum = jnp.cumsum(onehot, axis=0)
    counts = csum[-1]
    rank = jnp.sum(onehot * csum, axis=1) - 1
    padded = ((counts + tm - 1) // tm) * tm
    pad_end = jnp.cumsum(padded)
    pad_start = pad_end - padded
    start = jnp.cumsum(counts) - counts
    pos = (jnp.sum(onehot * pad_start[None, :], axis=1) + rank).reshape(N, TOP_K)
    R = n_slots + n_experts * tm
    tile_ids = jnp.arange(R // tm, dtype=jnp.int32)
    tile_expert = jnp.minimum(jnp.searchsorted(pad_end // tm, tile_ids, side="right"), n_experts - 1).astype(jnp.int32)
    n_tiles = (pad_end[-1] // tm).reshape(1).astype(jnp.int32)
    order = jnp.argsort(flat_e, stable=True).astype(jnp.int32)
    row_e = jnp.repeat(tile_expert, tm)
    row_rank = jnp.arange(R, dtype=jnp.int32) - pad_start[row_e]
    src_slot = order[jnp.clip(start[row_e] + row_rank, 0, n_slots - 1)]
    src_token = jnp.where(row_rank < counts[row_e], src_slot // TOP_K, 0)
    return w, src_token, pos, tile_expert, n_tiles


def _prep_weights(p):
    HW = p["w_branch_a"].shape[1]
    w_in = p["w_in"]
    n_f = p["b_fox_f"].shape[-1]
    q_end = 6 * HW
    wf = jnp.pad(w_in[:, :, q_end:q_end + n_f], ((0, 0), (0, 0), (0, LANES - n_f)))
    router = p["moe_router"]
    return dict(
        wqkv=w_in[:, :, :q_end].astype(BF16),
        wf=wf.astype(BF16),
        wgate=w_in[:, :, q_end + n_f:].astype(BF16),
        wa=p["w_branch_a"].astype(BF16), wb=p["w_branch_b"].astype(BF16), wo=p["w_out"].astype(BF16),
        ffn_g=p["ffn_w_gate"].astype(BF16), ffn_u=p["ffn_w_up"].astype(BF16), ffn_d=p["ffn_w_down"].astype(BF16),
        moe_g=p["moe_w_gate"].astype(BF16), moe_u=p["moe_w_up"].astype(BF16), moe_d=p["moe_w_down"].astype(BF16),
        router=jnp.pad(router, ((0, 0), (0, 0), (0, LANES - router.shape[-1]))),
    )


def _trunk(x, mod, p, w, attn_fn, *, tm, tm_moe, tf, emit_bf16, q_scale_a, q_scale_b, chunk):
    depth = p["w_in"].shape[0]
    n_experts = p["moe_router"].shape[-1]
    rows = [[], [], [], [], []]
    for l in range(depth):
        pr = _inproj(x, mod, l, p["norm_attn_g"][l][None], w["wqkv"][l], w["wf"][l], w["wgate"][l],
                     p["b_fox_f"][l][None], emit_bf16=emit_bf16, scale_a=q_scale_a, scale_b=q_scale_b, tm=tm,
                     chunk=chunk)
        qa, ka, va, qb, kb, vb, logf, sga, sgb = pr[:9]
        ya, yb = attn_fn(l, pr)
        x = _post_attn(ya, yb, sga, sgb, x, mod, l, w["wa"][l], w["wb"][l], w["wo"][l], tm=tm)
        final_g = p["final_norm_g"][None] if l == depth - 1 else None
        i = l // 2
        if l % 2 == 0:
            h, _ = _ffn_norm(x, mod, l, p["norm_ffn_g"][l][None], None, tm=tm)
            N = h.shape[0]
            tmd = _pick_tile(N, tm_moe)
            y = _ffn_grouped(h, jnp.zeros((N // tmd,), jnp.int32), jnp.full((1,), N // tmd, jnp.int32),
                             w["ffn_g"][i:i + 1], w["ffn_u"][i:i + 1], w["ffn_d"][i:i + 1], tm=tmd, tf=tf)
            x = _combine(x, mod, l, [y], [], final_g, tm=tm)
        else:
            h, logits = _ffn_norm(x, mod, l, p["norm_ffn_g"][l][None], w["router"][i], tm=tm)
            gate_w, src_token, pos, tile_expert, n_tiles = _route(logits[:, :n_experts], n_experts, tm_moe)
            y = _ffn_grouped(h, tile_expert, n_tiles, w["moe_g"][i], w["moe_u"][i], w["moe_d"][i],
                             tm=tm_moe, tf=tf, src_token=src_token)
            ys = [jnp.take(y, pos[:, k], axis=0, mode="clip") for k in range(TOP_K)]
            ws = [gate_w[:, k:k + 1] for k in range(TOP_K)]
            x = _combine(x, mod, l, ys, ws, final_g, tm=tm)
        for r, n in zip(rows, (ka, va, kb, vb, logf)):
            r.append(n)
    return x, [jnp.stack(r) for r in rows]


def kernel(x_prompt, x_sample, c_prompt, c_sample, cache_diff_k, cache_diff_v, cache_fox_k, cache_fox_v, cache_fox_logf, page_table, w_ada, b_ada, norm_attn_g, norm_ffn_g, w_in, b_fox_f, lambda_q1, lambda_k1, lambda_q2, lambda_k2, subln_g, rel_bias, w_branch_a, w_branch_b, w_out, ffn_w_gate, ffn_w_up, ffn_w_down, moe_router, moe_w_gate, moe_w_up, moe_w_down, final_norm_g):
    p = dict(w_ada=w_ada, b_ada=b_ada, norm_attn_g=norm_attn_g, norm_ffn_g=norm_ffn_g, w_in=w_in,
             b_fox_f=b_fox_f, subln_g=subln_g, rel_bias=rel_bias, w_branch_a=w_branch_a,
             w_branch_b=w_branch_b, w_out=w_out, ffn_w_gate=ffn_w_gate, ffn_w_up=ffn_w_up,
             ffn_w_down=ffn_w_down, moe_router=moe_router, moe_w_gate=moe_w_gate, moe_w_up=moe_w_up,
             moe_w_down=moe_w_down, final_norm_g=final_norm_g)
    B, T, D = x_prompt.shape
    Bs, S, _ = x_sample.shape
    depth, n_phys, page, HA, DK2 = cache_diff_k.shape
    HB, DH = cache_fox_k.shape[3:]
    DK = DK2 // 2
    n_pages = page_table.shape[1]
    past_len = n_pages * page
    W = HA * DK2
    assert DK2 == LANES and 2 * DH == LANES and HB * DH == W and cache_diff_v.shape[-1] == LANES
    G = W // LANES
    d_ff = ffn_w_gate.shape[-1]
    tf = d_ff // 2 if (d_ff // 2) % LANES == 0 else d_ff

    w = _prep_weights(p)
    bp = B + (-B) % SUBLANES
    mod_p, mod_s = _ada_mod(jnp.pad(c_prompt, ((0, bp - B), (0, 0))), c_sample, w_ada, b_ada)
    mod_p = _Mod(mod_p.reshape(-1, 1, D), T, nb=bp)
    mod_s = _Mod(jnp.repeat(mod_s, S, axis=1), 1)

    lam_init = np.array([0.8 - 0.6 * math.exp(-0.3 * l) for l in range(depth)], np.float32)
    lam = (jnp.exp(jnp.sum(lambda_q1 * lambda_k1, axis=-1)) - jnp.exp(jnp.sum(lambda_q2 * lambda_k2, axis=-1))
           + lam_init)

    t = _pick_tile(T, FLASH_TILE)
    key_ = np.arange(t)[:, None]
    qry_ = (np.arange(2 * t) % t)[None, :]
    far_b = np.unique(_t5_bucket_np(np.arange(t + 1, max(T, t + 2))))
    assert far_b.size == 1, "far key blocks must share one bias bucket"
    bkt = np.concatenate([_t5_bucket_np(qry_ - key_), _t5_bucket_np(t + qry_ - key_)], axis=0)
    c_far = rel_bias[int(far_b[0])]
    b01 = (_bias_lookup(rel_bias, bkt) - c_far[:, None, None]) * LOG2E
    b0, b1 = b01[:, :t], b01[:, t:]
    kpos = np.arange(past_len + page)[None, :]
    qpos = past_len + np.tile(np.arange(S), 2)[:, None]
    bias_s = _bias_lookup(rel_bias, _t5_bucket_np(qpos - kpos))

    def scal(l):
        return jnp.concatenate([c_far, lam[l][None], jnp.full((1,), 1.0 - float(lam_init[l]), F32)]).astype(F32)

    def attn_prompt(l, pr):
        qa, qb, logf = pr[0], pr[3], pr[6]
        kab, vab, kbb, vbb = pr[9:13]
        r3 = lambda a: a.reshape(B, T, W)
        rt = lambda a: a.reshape(B, T // t, W, t)
        ya = _flash_prompt("diff", rt(qa), r3(kab), rt(vab), scal(l), (b0, b1, subln_g[l][:, None]), t=t)
        cum = jnp.cumsum(logf.reshape(B, T, HB), axis=1)
        kx, qx = _decay_columns(cum.reshape(B, T, G, 2) * LOG2E)
        yb = _flash_prompt("fox", rt(qb), r3(kbb), rt(vbb), jnp.zeros((1,), F32), (kx, qx), t=t)
        return ya.reshape(B * T, W), yb.reshape(B * T, W)

    y_p, rows_p = _trunk(x_prompt.reshape(B * T, D), mod_p, p, w, attn_prompt,
                         tm=_pick_tile(t, 256), tm_moe=_pick_tile(T, 512), tf=tf, emit_bf16=True,
                         q_scale_a=DK ** -0.5 * LOG2E, q_scale_b=DH ** -0.5 * LOG2E, chunk=t)

    diff_view = lambda c: c.reshape(depth, n_phys, page * HA, DK2)
    fox_view = lambda c: c.transpose(0, 1, 3, 4, 2).reshape(depth, n_phys, HB * DH, page)
    cdk, cdv, cfk, cfv = diff_view(cache_diff_k), diff_view(cache_diff_v), fox_view(cache_fox_k), fox_view(cache_fox_v)

    def attn_sample(l, pr):
        qa, ka, va, qb, kb, vb, logf = pr[:7]
        ya = _attn_sample("diff", qa, ka, va, scal(l), (bias_s, subln_g[l][None]), cdk, cdv, l, page_table)
        past_lf = cache_fox_logf[l][page_table].reshape(Bs, past_len, HB)
        cum = jnp.cumsum(jnp.concatenate([past_lf, logf.reshape(Bs, S, HB)], axis=1), axis=1)
        cq = cum[:, past_len:]
        ck = jnp.pad(cum, ((0, 0), (0, page - S), (0, 0))).transpose(0, 2, 1)
        yb = _attn_sample("fox", qb, kb, vb, jnp.zeros((1,), F32), (cq, ck), cfk, cfv, l, page_table)
        return ya, yb

    Ns = Bs * S
    y_s, rows_s = _trunk(x_sample.reshape(Ns, D), mod_s, p, w, attn_sample,
                         tm=_pick_tile(Ns, 256), tm_moe=_pick_tile(Ns, 256), tf=tf, emit_bf16=False,
                         q_scale_a=DK ** -0.5, q_scale_b=DH ** -0.5, chunk=_pick_tile(Ns, 256))

    def shape_rows(rows, b, s):
        dk_, dv_, fk_, fv_, lf_ = rows
        return (dk_.reshape(depth, b, s, HA, DK2), dv_.reshape(depth, b, s, HA, LANES),
                fk_.reshape(depth, b, s, HB, DH), fv_.reshape(depth, b, s, HB, DH), lf_.reshape(depth, b, s, HB))

    return (y_p.reshape(B, T, D), y_s.reshape(Bs, S, D)) + shape_rows(rows_p, B, T) + shape_rows(rows_s, Bs, S)
```
